```python
import jax, jax.numpy as jnp
from jax import lax
import numpy as np

D_MODEL = 2048
BATCH = 4
SEQ = 2048
DEPTH = 1
DEC_BATCH = 16
DEC_SEQ = 32
PAST_LEN = 1024

CHUNK = 64
N_META = 16
EPS = 1e-5
H_A = 4
DV_A = D_MODEL // H_A
DK_A = DV_A // 2
QK_A = 2 * H_A * DK_A
CONV_W = 4
H_B = 8
DK_B = D_MODEL // H_B
DV_B = 2 * DK_B
ROPE_BASE = 10000.0
N_EXPERTS = 32
TOP_K = 4
D_FF = D_MODEL
SWIGLU_LIMIT = 7.0
SWIGLU_ALPHA = 1.702
ROWS_PER_BLOCK = 128
SPLIT_SIZES = (QK_A, H_A * DV_A, H_A, H_A, H_A * DV_A, H_B * DK_B, H_B * DK_B, H_B * DV_B, H_B * DV_B, D_MODEL, D_MODEL)
N_COLS = QK_A + 2 * H_A * DV_A + 2 * H_A + 2 * H_B * DK_B + 2 * H_B * DV_B + 2 * D_MODEL

kernel_name = 'hybrid_mlstm_retention_moe_stream_step'


def _rmsnorm(x, g):
    xf = x.astype(jnp.float32)
    y = xf * lax.rsqrt(jnp.mean(xf * xf, axis=-1, keepdims=True) + EPS)
    return (y * g.astype(jnp.float32)).astype(x.dtype)


def _rotary(x, pos):
    half = x.shape[-1] // 2
    inv = jnp.power(ROPE_BASE, -jnp.arange(half, dtype=jnp.float32) / half)
    ang = pos.astype(jnp.float32)[:, None] * inv[None, :]
    cos = jnp.cos(ang)[None, :, None, :]
    sin = jnp.sin(ang)[None, :, None, :]
    x1, x2 = x[..., :half], x[..., half:]
    return jnp.concatenate([x1 * cos - x2 * sin, x2 * cos + x1 * sin], axis=-1)


def _causal_conv(x, buf, w, b):
    T = x.shape[1]
    xp = jnp.concatenate([buf.astype(x.dtype), x], axis=1)
    out = b
    for j in range(CONV_W):
        out = out + w[j] * xp[:, j:j + T]
    return out, xp[:, -(CONV_W - 1):]


def _to_blocks(a, c):
    B, T = a.shape[0], a.shape[1]
    a = a.reshape((B, T // c, c) + a.shape[2:])
    a = jnp.moveaxis(a, 1, 0)
    return jnp.moveaxis(a, 3, 2)


def _from_blocks(o):
    o = jnp.moveaxis(jnp.moveaxis(o, 0, 1), 2, 3)
    return o.reshape((o.shape[0], o.shape[1] * o.shape[2]) + o.shape[3:])


def _run_segments(step, carry, seqs, segments):
    outs = []
    for start, length, c in segments:
        blocks = tuple(_to_blocks(a[:, start:start + length], c) for a in seqs)
        carry, o = lax.scan(step, carry, blocks)
        outs.append(_from_blocks(o))
    return carry, jnp.concatenate(outs, axis=1)


def _mlstm_step(carry, inp):
    C, n, m = carry
    q, k, v, ig, lf = inp
    c = q.shape[2]
    causal = jnp.tril(jnp.ones((c, c), dtype=bool))
    b = jnp.cumsum(lf, axis=-1)
    logD = b[..., :, None] - b[..., None, :] + ig[..., None, :]
    logD = jnp.where(causal, logD, -jnp.inf)
    inter = b + m[..., None]
    m_tok = jnp.maximum(inter, jnp.max(logD, axis=-1))
    dw = jnp.exp(logD - m_tok[..., None])
    a_inter = jnp.exp(inter - m_tok)
    s = jnp.einsum('bhtd,bhsd->bhts', q, k) * dw
    num = a_inter[..., None] * jnp.einsum('bhtd,bhde->bhte', q, C) + jnp.einsum('bhts,bhse->bhte', s, v)
    den = a_inter * jnp.einsum('bhtd,bhd->bht', q, n) + jnp.sum(s, axis=-1)
    den = jnp.maximum(jnp.abs(den), jnp.exp(-m_tok))
    h = num / den[..., None]
    bL = b[..., -1]
    logw = bL[..., None] - b + ig
    m_new = jnp.maximum(bL + m, jnp.max(logw, axis=-1))
    w = jnp.exp(logw - m_new[..., None])
    decay = jnp.exp(bL + m - m_new)
    C_new = decay[..., None, None] * C + jnp.einsum('bhs,bhsd,bhse->bhde', w, k, v)
    n_new = decay[..., None] * n + jnp.einsum('bhs,bhsd->bhd', w, k)
    return (C_new, n_new, m_new), h


def _ret_step(S, inp):
    q, k, v = inp
    c = q.shape[2]
    log_g = jnp.log1p(-jnp.power(2.0, -5.0 - jnp.arange(H_B, dtype=jnp.float32)))
    t = jnp.arange(c, dtype=jnp.float32)
    diff = t[:, None] - t[None, :]
    causal = diff >= 0
    dmat = jnp.where(causal, jnp.exp(log_g[:, None, None] * jnp.where(causal, diff, 0.0)), 0.0)
    scores = jnp.einsum('bhtd,bhsd->bhts', q, k) * dmat
    cross = jnp.exp(log_g[:, None] * (t + 1.0))[None, :, :, None]
    o = jnp.einsum('bhts,bhse->bhte', scores, v) + cross * jnp.einsum('bhtd,bhde->bhte', q, S)
    wk = jnp.exp(log_g[:, None] * (c - 1.0 - t))
    S_new = jnp.exp(log_g * c)[None, :, None, None] * S + jnp.einsum('hs,bhsd,bhse->bhde', wk, k, v)
    return S_new, o


def _moe(h, w_r, b_r, w_g, b_g, w_u, b_u, w_d, b_d):
    f32 = jnp.float32
    Bsz, T, D = h.shape
    N = Bsz * T
    M = N * TOP_K
    R = ROWS_PER_BLOCK
    xt = h.reshape(N, D)
    logits = (xt @ w_r + b_r).astype(f32)
    top_val, top_idx = lax.top_k(logits, TOP_K)
    wts = jax.nn.softmax(top_val, axis=-1)
    e_flat = top_idx.reshape(M).astype(jnp.int32)
    order = jnp.argsort(e_flat).astype(jnp.int32)
    e_sorted = e_flat[order]
    counts = jnp.bincount(e_flat, length=N_EXPERTS).astype(jnp.int32)
    starts = jnp.cumsum(counts) - counts
    padded = (counts + (R - 1)) // R * R
    pends = jnp.cumsum(padded)
    pstarts = pends - padded
    dest_sorted = pstarts[e_sorted] + jnp.arange(M, dtype=jnp.int32) - starts[e_sorted]
    n_blocks = (M + N_EXPERTS * (R - 1) + R - 1) // R
    row_tok = jnp.full((n_blocks * R,), N, jnp.int32).at[dest_sorted].set(order // TOP_K)
    blk_expert = jnp.minimum(jnp.searchsorted(pends, jnp.arange(n_blocks, dtype=jnp.int32) * R, side='right'), N_EXPERTS - 1).astype(jnp.int32)
    x_pad = jnp.concatenate([xt, jnp.zeros((1, D), xt.dtype)], axis=0)

    def expert_block(args):
        rows, e = args
        xb = x_pad[rows]
        g = jnp.minimum((xb @ w_g[e] + b_g[e]).astype(f32), SWIGLU_LIMIT)
        u = jnp.clip((xb @ w_u[e] + b_u[e]).astype(f32), -SWIGLU_LIMIT, SWIGLU_LIMIT)
        a = g * jax.nn.sigmoid(SWIGLU_ALPHA * g) * (u + 1.0)
        return a.astype(xb.dtype) @ w_d[e] + b_d[e]

    out_rows = lax.map(expert_block, (row_tok.reshape(n_blocks, R), blk_expert)).reshape(n_blocks * R, D)
    dest = jnp.zeros((M,), jnp.int32).at[order].set(dest_sorted)
    y = jnp.einsum('nk,nkd->nd', wts, out_rows[dest].reshape(N, TOP_K, D).astype(f32))
    return y.astype(h.dtype).reshape(Bsz, T, D)


def _layer(x, pos, segments, state, lw):
    (norm1, w_in, b_ig, b_fg, conv_w, conv_b, mnorm, rnorm, w_out_a, w_out_b, w_o, norm2,
     w_r, b_r, w_g, b_g, w_u, b_u, w_d, b_d) = lw
    C0, n0, m0, conv0, S0 = state
    f32 = jnp.float32
    Bsz, T, _ = x.shape
    h = _rmsnorm(x, norm1)
    z = h @ w_in
    idx = [int(v) for v in np.cumsum(SPLIT_SIZES)[:-1]]
    (qk_pre, v_a, i_pre, f_pre, o_a, q_b, k_b, v_b, g_b, gate_a, gate_b) = jnp.split(z, idx, axis=-1)

    qk, conv_new = _causal_conv(qk_pre, conv0, conv_w, conv_b)
    qk = jax.nn.silu(qk.astype(f32))
    q_a = qk[..., :QK_A // 2].reshape(Bsz, T, H_A, DK_A)
    k_a = qk[..., QK_A // 2:].reshape(Bsz, T, H_A, DK_A) * (DK_A ** -0.5)
    v_a = v_a.astype(f32).reshape(Bsz, T, H_A, DV_A)
    ig = (i_pre + b_ig).astype(f32)
    lf = jax.nn.log_sigmoid((f_pre + b_fg).astype(f32))
    (C, n, m), h_a = _run_segments(_mlstm_step, (C0.astype(f32), n0.astype(f32), m0.astype(f32)),
                                   (q_a, k_a, v_a, ig, lf), segments)
    h_a = h_a * lax.rsqrt(jnp.mean(h_a * h_a, axis=-1, keepdims=True) + EPS) * mnorm.astype(f32).reshape(H_A, DV_A)
    h_a = h_a.reshape(Bsz, T, H_A * DV_A) * jax.nn.sigmoid(o_a.astype(f32))
    y_a = h_a.astype(x.dtype) @ w_out_a

    q_r = _rotary(q_b.astype(f32).reshape(Bsz, T, H_B, DK_B), pos)
    k_r = _rotary(k_b.astype(f32).reshape(Bsz, T, H_B, DK_B), pos) * (DK_B ** -0.5)
    v_r = v_b.astype(f32).reshape(Bsz, T, H_B, DV_B)
    S, o_b = _run_segments(_ret_step, S0.astype(f32), (q_r, k_r, v_r), segments)
    mu = jnp.mean(o_b, axis=-1, keepdims=True)
    var = jnp.mean(jnp.square(o_b - mu), axis=-1, keepdims=True)
    o_b = (o_b - mu) * lax.rsqrt(var + EPS) * rnorm.astype(f32).reshape(H_B, DV_B)
    o_b = o_b.reshape(Bsz, T, H_B * DV_B) * jax.nn.silu(g_b.astype(f32))
    y_b = o_b.astype(x.dtype) @ w_out_b

    merged = jax.nn.sigmoid(gate_a.astype(f32)) * y_a.astype(f32) + jax.nn.sigmoid(gate_b.astype(f32)) * y_b.astype(f32)
    x = x + merged.astype(x.dtype) @ w_o
    x = x + _moe(_rmsnorm(x, norm2), w_r, b_r, w_g, b_g, w_u, b_u, w_d, b_d)
    return x, (C, n, m, conv_new, S)


def setup_inputs(seed: int = 0) -> dict:
    key = jax.random.key(seed)
    keys = list(jax.random.split(key, 40))

    def rnd(shape, scale):
        return jax.random.normal(keys.pop(), shape, jnp.float32) * scale

    D = D_MODEL
    return {
        'x_prompt': rnd((BATCH, SEQ, D), 1.0),
        'x_sample': rnd((DEC_BATCH, DEC_SEQ, D), 1.0),
        'state_mlstm_C': rnd((DEPTH, DEC_BATCH, H_A, DK_A, DV_A), 0.05),
        'state_mlstm_n': rnd((DEPTH, DEC_BATCH, H_A, DK_A), 0.05),
        'state_mlstm_m': rnd((DEPTH, DEC_BATCH, H_A), 1.0),
        'state_mlstm_conv': rnd((DEPTH, DEC_BATCH, CONV_W - 1, QK_A), 1.0),
        'state_ret': rnd((DEPTH, DEC_BATCH, H_B, DK_B, DV_B), 0.05),
        'meta_tokens': rnd((N_META, D), 1.0),
        'norm1': 1.0 + rnd((DEPTH, D), 0.02),
        'w_in': rnd((DEPTH, D, N_COLS), D ** -0.5),
        'b_igate': rnd((DEPTH, H_A), 0.1),
        'b_fgate': jnp.linspace(3.0, 6.0, H_A, dtype=jnp.float32)[None, :] + rnd((DEPTH, H_A), 0.1),
        'conv_w': rnd((DEPTH, CONV_W, QK_A), CONV_W ** -0.5),
        'conv_b': rnd((DEPTH, QK_A), 0.01),
        'mlstm_norm': 1.0 + rnd((DEPTH, H_A * DV_A), 0.02),
        'ret_norm': 1.0 + rnd((DEPTH, H_B * DV_B), 0.02),
        'w_out_a': rnd((DEPTH, H_A * DV_A, D), (H_A * DV_A) ** -0.5),
        'w_out_b': rnd((DEPTH, H_B * DV_B, D), (H_B * DV_B) ** -0.5),
        'w_o': rnd((DEPTH, D, D), D ** -0.5),
        'norm2': 1.0 + rnd((DEPTH, D), 0.02),
        'w_router': rnd((DEPTH, D, N_EXPERTS), D ** -0.5),
        'b_router': rnd((DEPTH, N_EXPERTS), 0.01),
        'w_gate': rnd((DEPTH, N_EXPERTS, D, D_FF), D ** -0.5),
        'b_gate': rnd((DEPTH, N_EXPERTS, D_FF), 0.01),
        'w_up': rnd((DEPTH, N_EXPERTS, D, D_FF), D ** -0.5),
        'b_up': rnd((DEPTH, N_EXPERTS, D_FF), 0.01),
        'w_down': rnd((DEPTH, N_EXPERTS, D_FF, D), D_FF ** -0.5),
        'b_down': rnd((DEPTH, N_EXPERTS, D), 0.01),
        'final_norm': 1.0 + rnd((D,), 0.02),
    }


def reference(x_prompt, x_sample, state_mlstm_C, state_mlstm_n, state_mlstm_m, state_mlstm_conv, state_ret,
              meta_tokens, norm1, w_in, b_igate, b_fgate, conv_w, conv_b, mlstm_norm, ret_norm,
              w_out_a, w_out_b, w_o, norm2, w_router, b_router, w_gate, b_gate, w_up, b_up,
              w_down, b_down, final_norm):
    f32 = jnp.float32
    dt_p = x_prompt.dtype
    dt_s = state_mlstm_C.dtype
    Bp, Tp = x_prompt.shape[0], x_prompt.shape[1]
    Ts = x_sample.shape[1]
    xp = jnp.concatenate([jnp.broadcast_to(meta_tokens.astype(dt_p)[None], (Bp, N_META, D_MODEL)), x_prompt], axis=1)
    pos_p = jnp.arange(N_META + Tp, dtype=jnp.int32)
    seg_p = ((0, N_META, N_META), (N_META, Tp, CHUNK))
    xs = x_sample
    pos_s = N_META + PAST_LEN + jnp.arange(Ts, dtype=jnp.int32)
    seg_s = ((0, Ts, Ts),)
    st_p, st_s = [], []
    for l in range(DEPTH):
        lw = (norm1[l], w_in[l], b_igate[l], b_fgate[l], conv_w[l], conv_b[l], mlstm_norm[l], ret_norm[l],
              w_out_a[l], w_out_b[l], w_o[l], norm2[l], w_router[l], b_router[l], w_gate[l], b_gate[l],
              w_up[l], b_up[l], w_down[l], b_down[l])
        init_p = (jnp.zeros((Bp, H_A, DK_A, DV_A), f32), jnp.zeros((Bp, H_A, DK_A), f32), jnp.zeros((Bp, H_A), f32),
                  jnp.zeros((Bp, CONV_W - 1, QK_A), dt_p), jnp.zeros((Bp, H_B, DK_B, DV_B), f32))
        xp, sp = _layer(xp, pos_p, seg_p, init_p, lw)
        xs, ss = _layer(xs, pos_s, seg_s, (state_mlstm_C[l], state_mlstm_n[l], state_mlstm_m[l],
                                           state_mlstm_conv[l], state_ret[l]), lw)
        st_p.append(sp)
        st_s.append(ss)
    y_prompt = _rmsnorm(xp[:, N_META:], final_norm)
    y_sample = _rmsnorm(xs, final_norm)
    p_C = jnp.stack([s[0] for s in st_p]).astype(dt_p)
    p_n = jnp.stack([s[1] for s in st_p]).astype(dt_p)
    p_m = jnp.stack([s[2] for s in st_p]).astype(dt_p)
    p_conv = jnp.stack([s[3] for s in st_p]).astype(dt_p)
    p_ret = jnp.stack([s[4] for s in st_p]).astype(dt_p)
    s_C = jnp.stack([s[0] for s in st_s]).astype(dt_s)
    s_n = jnp.stack([s[1] for s in st_s]).astype(dt_s)
    s_m = jnp.stack([s[2] for s in st_s]).astype(dt_s)
    s_conv = jnp.stack([s[3] for s in st_s]).astype(dt_s)
    s_ret = jnp.stack([s[4] for s in st_s]).astype(dt_s)
    return (y_prompt, y_sample, p_C, p_n, p_m, p_conv, p_ret, s_C, s_n, s_m, s_conv, s_ret)
```

```python
import functools

import numpy as np
import jax
import jax.numpy as jnp
from jax import lax
from jax.experimental import pallas as pl
from jax.experimental.pallas import tpu as pltpu

F32 = jnp.float32
BF16 = jnp.bfloat16

EPS = 1e-5
N_META = 16
PAST_LEN = 1024
H_A = 4
H_B = 8
CONV_W = 4
ROPE_BASE = 10000.0
N_EXPERTS = 32
TOP_K = 4
SWIGLU_LIMIT = 7.0
SWIGLU_ALPHA = 1.702

LANES = 128
SUBLANES = 8
VMEM_LIMIT = 56 * 1024 * 1024
PROMPT_CHUNK = 256
EXPERT_ROWS = 256
COMBINE_ROWS = 128


def _params(sem, vmem=VMEM_LIMIT):
    return pltpu.CompilerParams(dimension_semantics=sem, vmem_limit_bytes=vmem)


def _dot(a, b):
    return jnp.dot(a, b, preferred_element_type=F32)


def _dot_nt(a, b):
    return lax.dot_general(a, b, (((1,), (1,)), ((), ())), preferred_element_type=F32)


def _dot_tn(a, b):
    return lax.dot_general(a, b, (((0,), (0,)), ((), ())), preferred_element_type=F32)


def _split_bf16(w):
    hi = w.astype(BF16)
    lo = (w - hi.astype(F32)).astype(BF16)
    return hi, lo


def _dot3(a, w_hi, w_lo):
    a_hi = a.astype(BF16)
    a_lo = (a - a_hi.astype(F32)).astype(BF16)
    return _dot(a_hi, w_hi) + _dot(a_hi, w_lo) + _dot(a_lo, w_hi)


def _norm_gates_body(x_ref, g_ref, whi_ref, wlo_ref, b_ref, h_ref, gates_ref):
    x = x_ref[...]
    y = x * lax.rsqrt(jnp.mean(x * x, axis=-1, keepdims=True) + EPS) * g_ref[...]
    h_ref[...] = y.astype(BF16)
    pre = _dot3(y, whi_ref[...], wlo_ref[...]) + b_ref[...]
    lane = lax.broadcasted_iota(jnp.int32, pre.shape, 1)
    logsig = jnp.minimum(pre, 0.0) - jnp.log1p(jnp.exp(-jnp.abs(pre)))
    gates_ref[...] = jnp.where(lane < H_A, pre, logsig)


def _norm_gates(x, gain, w_hi, w_lo, bias, tm):
    m, d = x.shape
    return pl.pallas_call(
        _norm_gates_body,
        grid=(m // tm,),
        in_specs=[
            pl.BlockSpec((tm, d), lambda i: (i, 0)),
            pl.BlockSpec((1, d), lambda i: (0, 0)),
            pl.BlockSpec((d, LANES), lambda i: (0, 0)),
            pl.BlockSpec((d, LANES), lambda i: (0, 0)),
            pl.BlockSpec((1, LANES), lambda i: (0, 0)),
        ],
        out_specs=[
            pl.BlockSpec((tm, d), lambda i: (i, 0)),
            pl.BlockSpec((tm, LANES), lambda i: (i, 0)),
        ],
        out_shape=[
            jax.ShapeDtypeStruct((m, d), BF16),
            jax.ShapeDtypeStruct((m, LANES), F32),
        ],
        compiler_params=_params(("parallel",)),
        name="norm_gates",
    )(x, gain, w_hi, w_lo, bias)


def _mm_body(x_ref, w_ref, o_ref):
    o_ref[...] = _dot(x_ref[...], w_ref[...]).astype(o_ref.dtype)


def _in_proj(h, w, tm, tn, ncol_blocks=None):
    m, k = h.shape
    n = w.shape[1]
    nj = n // tn if ncol_blocks is None else ncol_blocks
    return pl.pallas_call(
        _mm_body,
        grid=(nj, m // tm),
        in_specs=[
            pl.BlockSpec((tm, k), lambda j, i: (i, 0)),
            pl.BlockSpec((k, tn), lambda j, i: (0, j)),
        ],
        out_specs=pl.BlockSpec((tm, tn), lambda j, i: (i, j)),
        out_shape=jax.ShapeDtypeStruct((m, n), F32),
        compiler_params=_params(("parallel", "parallel")),
        name="in_proj",
    )(h, w)


def _causal_conv_silu(buf, x_ref, w_ref, b_ref, c):
    buf[SUBLANES:SUBLANES + c, :] = x_ref[...]
    acc = b_ref[...] + w_ref[0:1, :] * buf[5:5 + c, :]
    acc = acc + w_ref[1:2, :] * buf[6:6 + c, :]
    acc = acc + w_ref[2:3, :] * buf[7:7 + c, :]
    acc = acc + w_ref[3:4, :] * buf[8:8 + c, :]
    buf[0:SUBLANES, :] = buf[c:c + SUBLANES, :]
    return acc * jax.nn.sigmoid(acc)


def _mlstm_body(c, dk, nchunks, qpre_ref, kpre_ref, v_ref, oa_ref, gcol_ref, grow_ref,
                cq0_ref, ck0_ref, cwq_ref, cwk_ref, cbq_ref, cbk_ref, mn_ref,
                c0_ref, n0_ref, m0_ref, *rest):
    ha_ref, cout_ref, nout_ref, mout_ref, c_s, n_s, m_s, qbuf, kbuf = rest[-9:]
    h = pl.program_id(1)
    j = pl.program_id(2)

    @pl.when(j == 0)
    def _():
        c_s[...] = c0_ref[0, 0]
        n_s[...] = n0_ref[0, 0]
        m_s[...] = m0_ref[0, 0]
        qbuf[0:SUBLANES, :] = cq0_ref[0]
        kbuf[0:SUBLANES, :] = ck0_ref[0]

    q = _causal_conv_silu(qbuf, qpre_ref, cwq_ref, cbq_ref, c)
    k = _causal_conv_silu(kbuf, kpre_ref, cwk_ref, cbk_ref, c) * (dk ** -0.5)

    gc = gcol_ref[...]
    lane = lax.broadcasted_iota(jnp.int32, gc.shape, 1)
    ig_col = jnp.sum(jnp.where(lane == h, gc, 0.0), axis=1, keepdims=True)
    lf_col = jnp.sum(jnp.where(lane == h + H_A, gc, 0.0), axis=1, keepdims=True)
    gr = grow_ref[0]
    sub = lax.broadcasted_iota(jnp.int32, gr.shape, 0)
    ig_row = jnp.sum(jnp.where(sub == h, gr, 0.0), axis=0, keepdims=True)
    lf_row = jnp.sum(jnp.where(sub == h + H_A, gr, 0.0), axis=0, keepdims=True)

    t_i = lax.broadcasted_iota(jnp.int32, (c, c), 0)
    s_i = lax.broadcasted_iota(jnp.int32, (c, c), 1)
    causal = s_i <= t_i
    b_col = jnp.sum(jnp.where(causal, lf_row, 0.0), axis=1, keepdims=True)
    b_row = jnp.sum(jnp.where(t_i <= s_i, lf_col, 0.0), axis=0, keepdims=True)
    m_prev = m_s[0:1, 0:1]
    log_d = jnp.where(causal, b_col - b_row + ig_row, -jnp.inf)
    inter = b_col + m_prev
    m_tok = jnp.maximum(inter, jnp.max(log_d, axis=1, keepdims=True))
    dw = jnp.exp(log_d - m_tok)
    a_inter = jnp.exp(inter - m_tok)

    qb = q.astype(BF16)
    kb = k.astype(BF16)
    vb = v_ref[...].astype(BF16)
    s = _dot_nt(qb, kb) * dw
    c_old = c_s[...]
    num = a_inter * _dot(qb, c_old.astype(BF16)) + _dot(s.astype(BF16), vb)
    den = a_inter * jnp.sum(q * n_s[...], axis=1, keepdims=True) + jnp.sum(s, axis=1, keepdims=True)
    den = jnp.maximum(jnp.abs(den), jnp.exp(-m_tok))
    hh = num * (1.0 / den)

    b_last = jnp.sum(lf_row, axis=1, keepdims=True)
    logw = b_last - b_col + ig_col
    m_new = jnp.maximum(b_last + m_prev, jnp.max(logw, axis=0, keepdims=True))
    kw = k * jnp.exp(logw - m_new)
    decay = jnp.exp(b_last + m_prev - m_new)
    c_s[...] = decay * c_old + _dot_tn(kw.astype(BF16), vb)
    n_s[...] = decay * n_s[...] + jnp.sum(kw, axis=0, keepdims=True)
    m_s[...] = jnp.broadcast_to(m_new, m_s.shape)

    hn = hh * lax.rsqrt(jnp.mean(hh * hh, axis=-1, keepdims=True) + EPS) * mn_ref[...]
    ha_ref[...] = (hn * jax.nn.sigmoid(oa_ref[...])).astype(BF16)

    @pl.when(j == nchunks - 1)
    def _():
        cout_ref[0, 0] = c_s[...]
        nout_ref[0, 0] = n_s[...]
        mout_ref[0, 0] = m_s[...]


_DK = 256
_DV = 512
_ZC_QK, _ZC_VA, _ZC_OA, _ZC_QB, _ZC_KB, _ZC_VB, _ZC_GB, _ZC_GA, _ZC_GBM = (
    0, 2048, 4096, 6144, 8192, 10240, 14336, 18432, 20480)
_Z_COLS = 22528


def _mlstm_scan(z, gates, grow, conv0, conv_w8, conv_b, mnorm, c0, n0, m0, *, row0, gate_row0, nb, t, c,
                share_state, ha_prev, m_total):
    nchunks = t // c
    zr = row0 // c
    gr0 = gate_row0 // c
    sb = (lambda b: 0) if share_state else (lambda b: b)
    in_specs = [
        pl.BlockSpec((c, _DK), lambda b, h, j: (zr + b * nchunks + j, _ZC_QK // _DK + h)),
        pl.BlockSpec((c, _DK), lambda b, h, j: (zr + b * nchunks + j, _ZC_QK // _DK + H_A + h)),
        pl.BlockSpec((c, _DV), lambda b, h, j: (zr + b * nchunks + j, _ZC_VA // _DV + h)),
        pl.BlockSpec((c, _DV), lambda b, h, j: (zr + b * nchunks + j, _ZC_OA // _DV + h)),
        pl.BlockSpec((c, LANES), lambda b, h, j: (gr0 + b * nchunks + j, 0)),
        pl.BlockSpec((1, SUBLANES, c), lambda b, h, j: (b, 0, j)),
        pl.BlockSpec((1, SUBLANES, _DK), lambda b, h, j: (sb(b), 0, h)),
        pl.BlockSpec((1, SUBLANES, _DK), lambda b, h, j: (sb(b), 0, H_A + h)),
        pl.BlockSpec((SUBLANES, _DK), lambda b, h, j: (0, h)),
        pl.BlockSpec((SUBLANES, _DK), lambda b, h, j: (0, H_A + h)),
        pl.BlockSpec((1, _DK), lambda b, h, j: (0, h)),
        pl.BlockSpec((1, _DK), lambda b, h, j: (0, H_A + h)),
        pl.BlockSpec((1, _DV), lambda b, h, j: (0, h)),
        pl.BlockSpec((1, 1, _DK, _DV), lambda b, h, j: (sb(b), h, 0, 0)),
        pl.BlockSpec((1, 1, 1, _DK), lambda b, h, j: (sb(b), h, 0, 0)),
        pl.BlockSpec((1, 1, SUBLANES, LANES), lambda b, h, j: (sb(b), h, 0, 0)),
    ]
    args = [z, z, z, z, gates, grow, conv0, conv0, conv_w8, conv_w8, conv_b, conv_b, mnorm, c0, n0, m0]
    aliases = {}
    if ha_prev is not None:
        in_specs.append(pl.BlockSpec(memory_space=pl.ANY))
        args.append(ha_prev)
        aliases = {len(args) - 1: 0}
    out_specs = [
        pl.BlockSpec((c, _DV), lambda b, h, j: (gr0 + b * nchunks + j, h)),
        pl.BlockSpec((1, 1, _DK, _DV), lambda b, h, j: (b, h, 0, 0)),
        pl.BlockSpec((1, 1, 1, _DK), lambda b, h, j: (b, h, 0, 0)),
        pl.BlockSpec((1, 1, SUBLANES, LANES), lambda b, h, j: (b, h, 0, 0)),
    ]
    out_shape = [
        jax.ShapeDtypeStruct((m_total, H_A * _DV), BF16),
        jax.ShapeDtypeStruct((nb, H_A, _DK, _DV), F32),
        jax.ShapeDtypeStruct((nb, H_A, 1, _DK), F32),
        jax.ShapeDtypeStruct((nb, H_A, SUBLANES, LANES), F32),
    ]
    return pl.pallas_call(
        functools.partial(_mlstm_body, c, _DK, nchunks),
        grid=(nb, H_A, nchunks),
        in_specs=in_specs,
        out_specs=out_specs,
        out_shape=out_shape,
        scratch_shapes=[
            pltpu.VMEM((_DK, _DV), F32),
            pltpu.VMEM((1, _DK), F32),
            pltpu.VMEM((SUBLANES, LANES), F32),
            pltpu.VMEM((c + SUBLANES, _DK), F32),
            pltpu.VMEM((c + SUBLANES, _DK), F32),
        ],
        input_output_aliases=aliases,
        compiler_params=_params(("parallel", "parallel", "arbitrary")),
        name="mlstm_scan",
    )(*args)


def _rotary(x, cos, sin):
    half = x.shape[-1] // 2
    x1 = x[:, :half]
    x2 = x[:, half:]
    return jnp.concatenate([x1 * cos - x2 * sin, x2 * cos + x1 * sin], axis=-1)


def _ret_body(c, dk, nchunks, lg_ref, q_ref, k_ref, v_ref, g_ref, cos_ref, sin_ref, rn_ref, s0_ref, *rest):
    ob_ref, sout_ref, s_s = rest[-3:]
    h = pl.program_id(1)
    j = pl.program_id(2)

    @pl.when(j == 0)
    def _():
        s_s[...] = s0_ref[0, 0]

    lg = lg_ref[h]
    cos = cos_ref[...]
    sin = sin_ref[...]
    qb = _rotary(q_ref[...], cos, sin).astype(BF16)
    kr = _rotary(k_ref[...], cos, sin) * (dk ** -0.5)
    vb = v_ref[...].astype(BF16)

    t_i = lax.broadcasted_iota(jnp.int32, (c, c), 0)
    s_i = lax.broadcasted_iota(jnp.int32, (c, c), 1)
    causal = t_i >= s_i
    diff = jnp.where(causal, t_i - s_i, 0).astype(F32)
    dmat = jnp.where(causal, jnp.exp(lg * diff), 0.0)
    t_col = lax.broadcasted_iota(jnp.int32, (c, 1), 0).astype(F32)

    scores = _dot_nt(qb, kr.astype(BF16)) * dmat
    s_old = s_s[...]
    cross = jnp.exp(lg * (t_col + 1.0))
    o = _dot(scores.astype(BF16), vb) + cross * _dot(qb, s_old.astype(BF16))
    wk = jnp.exp(lg * (c - 1.0 - t_col))
    s_s[...] = jnp.exp(lg * c) * s_old + _dot_tn((kr * wk).astype(BF16), vb)

    mu = jnp.mean(o, axis=-1, keepdims=True)
    oc = o - mu
    var = jnp.mean(oc * oc, axis=-1, keepdims=True)
    g = g_ref[...]
    ob_ref[...] = (oc * lax.rsqrt(var + EPS) * rn_ref[...] * (g * jax.nn.sigmoid(g))).astype(BF16)

    @pl.when(j == nchunks - 1)
    def _():
        sout_ref[0, 0] = s_s[...]


def _ret_scan(log_g, z, cos, sin, rnorm, s0, *, row0, out_row0, nb, t, c, share_state, ob_prev, m_total):
    nchunks = t // c
    zr = row0 // c
    or0 = out_row0 // c
    sb = (lambda b: 0) if share_state else (lambda b: b)
    in_specs = [
        pl.BlockSpec((c, _DK), lambda b, h, j, lg: (zr + b * nchunks + j, _ZC_QB // _DK + h)),
        pl.BlockSpec((c, _DK), lambda b, h, j, lg: (zr + b * nchunks + j, _ZC_KB // _DK + h)),
        pl.BlockSpec((c, _DV), lambda b, h, j, lg: (zr + b * nchunks + j, _ZC_VB // _DV + h)),
        pl.BlockSpec((c, _DV), lambda b, h, j, lg: (zr + b * nchunks + j, _ZC_GB // _DV + h)),
        pl.BlockSpec((c, _DK // 2), lambda b, h, j, lg: (j, 0)),
        pl.BlockSpec((c, _DK // 2), lambda b, h, j, lg: (j, 0)),
        pl.BlockSpec((1, _DV), lambda b, h, j, lg: (0, h)),
        pl.BlockSpec((1, 1, _DK, _DV), lambda b, h, j, lg: (sb(b), h, 0, 0)),
    ]
    args = [z, z, z, z, cos, sin, rnorm, s0]
    aliases = {}
    if ob_prev is not None:
        in_specs.append(pl.BlockSpec(memory_space=pl.ANY))
        args.append(ob_prev)
        aliases = {len(args): 0}
    grid_spec = pltpu.PrefetchScalarGridSpec(
        num_scalar_prefetch=1,
        grid=(nb, H_B, nchunks),
        in_specs=in_specs,
        out_specs=[
            pl.BlockSpec((c, _DV), lambda b, h, j, lg: (or0 + b * nchunks + j, h)),
            pl.BlockSpec((1, 1, _DK, _DV), lambda b, h, j, lg: (b, h, 0, 0)),
        ],
        scratch_shapes=[pltpu.VMEM((_DK, _DV), F32)],
    )
    return pl.pallas_call(
        functools.partial(_ret_body, c, _DK, nchunks),
        grid_spec=grid_spec,
        out_shape=[
            jax.ShapeDtypeStruct((m_total, H_B * _DV), BF16),
            jax.ShapeDtypeStruct((nb, H_B, _DK, _DV), F32),
        ],
        input_output_aliases=aliases,
        compiler_params=_params(("parallel", "parallel", "arbitrary")),
        name="ret_scan",
    )(log_g, *args)


def _merge_body(ha_ref, ob_ref, wa_ref, wb_ref, ga_ref, gb_ref, o_ref):
    ya = _dot(ha_ref[...], wa_ref[...])
    yb = _dot(ob_ref[...], wb_ref[...])
    o_ref[...] = (jax.nn.sigmoid(ga_ref[...]) * ya + jax.nn.sigmoid(gb_ref[...]) * yb).astype(o_ref.dtype)


def _merge(ha, ob, wa, wb, z, tm, tn):
    m = ha.shape[0]
    d = wa.shape[1]
    return pl.pallas_call(
        _merge_body,
        grid=(d // tn, m // tm),
        in_specs=[
            pl.BlockSpec((tm, ha.shape[1]), lambda j, i: (i, 0)),
            pl.BlockSpec((tm, ob.shape[1]), lambda j, i: (i, 0)),
            pl.BlockSpec((wa.shape[0], tn), lambda j, i: (0, j)),
            pl.BlockSpec((wb.shape[0], tn), lambda j, i: (0, j)),
            pl.BlockSpec((tm, tn), lambda j, i: (i, _ZC_GA // tn + j)),
            pl.BlockSpec((tm, tn), lambda j, i: (i, _ZC_GBM // tn + j)),
        ],
        out_specs=pl.BlockSpec((tm, tn), lambda j, i: (i, j)),
        out_shape=jax.ShapeDtypeStruct((m, d), BF16),
        compiler_params=_params(("parallel", "parallel")),
        name="merge",
    )(ha, ob, wa, wb, z, z)


def _wo_router_body(mg_ref, wo_ref, x_ref, g2_ref, wrh_ref, wrl_ref, br_ref, x1_ref, h2_ref, idx_ref, wt_ref):
    x1 = x_ref[...] + _dot(mg_ref[...], wo_ref[...])
    x1_ref[...] = x1
    h2 = x1 * lax.rsqrt(jnp.mean(x1 * x1, axis=-1, keepdims=True) + EPS) * g2_ref[...]
    h2_ref[...] = h2
    logits = _dot3(h2, wrh_ref[...], wrl_ref[...]) + br_ref[...]
    lane = lax.broadcasted_iota(jnp.int32, logits.shape, 1)
    cur = jnp.where(lane < N_EXPERTS, logits, -jnp.inf)
    vals, idxs = [], []
    for _ in range(TOP_K):
        mx = jnp.max(cur, axis=1, keepdims=True)
        ix = jnp.min(jnp.where(cur == mx, lane, LANES), axis=1, keepdims=True)
        vals.append(mx)
        idxs.append(ix)
        cur = jnp.where(lane == ix, -jnp.inf, cur)
    exps = [jnp.exp(v - vals[0]) for v in vals]
    tot = exps[0] + exps[1] + exps[2] + exps[3]
    idx_out = jnp.zeros(logits.shape, jnp.int32)
    wt_out = jnp.zeros(logits.shape, F32)
    for kk in range(TOP_K):
        idx_out = jnp.where(lane == kk, idxs[kk], idx_out)
        wt_out = jnp.where(lane == kk, exps[kk] / tot, wt_out)
    idx_ref[...] = idx_out
    wt_ref[...] = wt_out


def _wo_router(merged, wo, x, g2, wr_hi, wr_lo, br, tm):
    m, d = x.shape
    return pl.pallas_call(
        _wo_router_body,
        grid=(m // tm,),
        in_specs=[
            pl.BlockSpec((tm, d), lambda i: (i, 0)),
            pl.BlockSpec((d, d), lambda i: (0, 0)),
            pl.BlockSpec((tm, d), lambda i: (i, 0)),
            pl.BlockSpec((1, d), lambda i: (0, 0)),
            pl.BlockSpec((d, LANES), lambda i: (0, 0)),
            pl.BlockSpec((d, LANES), lambda i: (0, 0)),
            pl.BlockSpec((1, LANES), lambda i: (0, 0)),
        ],
        out_specs=[
            pl.BlockSpec((tm, d), lambda i: (i, 0)),
            pl.BlockSpec((tm, d), lambda i: (i, 0)),
            pl.BlockSpec((tm, LANES), lambda i: (i, 0)),
            pl.BlockSpec((tm, LANES), lambda i: (i, 0)),
        ],
        out_shape=[
            jax.ShapeDtypeStruct((m, d), F32),
            jax.ShapeDtypeStruct((m, d), F32),
            jax.ShapeDtypeStruct((m, LANES), jnp.int32),
            jax.ShapeDtypeStruct((m, LANES), F32),
        ],
        compiler_params=_params(("parallel",)),
        name="wo_router",
    )(merged, wo, x, g2, wr_hi, wr_lo, br)


def _gather_body(na_ref, tok_ref, src_hbm, dst_hbm, sem):
    r = pl.program_id(0)
    rows = tok_ref.shape[-1]

    def row_copy(i):
        return pltpu.make_async_copy(src_hbm.at[pl.ds(tok_ref[0, 0, i], 1)],
                                     dst_hbm.at[pl.ds(r * rows + i, 1)], sem)

    @pl.when(r < na_ref[0])
    def _():
        def start(i, carry):
            row_copy(i).start()
            return carry
        lax.fori_loop(0, rows, start, 0)

        def wait(i, carry):
            row_copy(i).wait()
            return carry
        lax.fori_loop(0, rows, wait, 0)


def _gather_rows(n_active, row_tok, src, rows):
    nblk = row_tok.shape[0]
    d = src.shape[1]
    grid_spec = pltpu.PrefetchScalarGridSpec(
        num_scalar_prefetch=1,
        grid=(nblk,),
        in_specs=[
            pl.BlockSpec((1, 1, rows), lambda r, na: (r, 0, 0), memory_space=pltpu.SMEM),
            pl.BlockSpec(memory_space=pl.ANY),
        ],
        out_specs=pl.BlockSpec(memory_space=pl.ANY),
        scratch_shapes=[pltpu.SemaphoreType.DMA(())],
    )
    return pl.pallas_call(
        _gather_body,
        grid_spec=grid_spec,
        out_shape=jax.ShapeDtypeStruct((nblk * rows, d), src.dtype),
        compiler_params=_params(("arbitrary",)),
        name="moe_gather",
    )(n_active, row_tok, src)


def _expert_up_body(be_ref, na_ref, x_ref, wg_ref, wu_ref, bg_ref, bu_ref, a_ref, wg_s, wu_s):
    r = pl.program_id(1)
    active = r < na_ref[0]
    first = jnp.logical_or(r == 0, be_ref[r] != be_ref[jnp.maximum(r - 1, 0)])

    @pl.when(jnp.logical_and(active, first))
    def _():
        wg_s[...] = wg_ref[0].astype(BF16)
        wu_s[...] = wu_ref[0].astype(BF16)

    @pl.when(active)
    def _():
        x = x_ref[...].astype(BF16)
        g = jnp.minimum(_dot(x, wg_s[...]) + bg_ref[0], SWIGLU_LIMIT)
        u = jnp.clip(_dot(x, wu_s[...]) + bu_ref[0], -SWIGLU_LIMIT, SWIGLU_LIMIT)
        a_ref[...] = (g * jax.nn.sigmoid(SWIGLU_ALPHA * g) * (u + 1.0)).astype(BF16)


def _expert_up(blk_expert, n_active, xs, w_gate, w_up, b_gate, b_up, rows, tf):
    p, d = xs.shape
    dff = w_gate.shape[2]
    nblk = p // rows

    def rr(r, na):
        return jnp.minimum(r, na[0] - 1)

    grid_spec = pltpu.PrefetchScalarGridSpec(
        num_scalar_prefetch=2,
        grid=(dff // tf, nblk),
        in_specs=[
            pl.BlockSpec((rows, d), lambda f, r, be, na: (rr(r, na), 0)),
            pl.BlockSpec((1, d, tf), lambda f, r, be, na: (be[rr(r, na)], 0, f)),
            pl.BlockSpec((1, d, tf), lambda f, r, be, na: (be[rr(r, na)], 0, f)),
            pl.BlockSpec((1, 1, tf), lambda f, r, be, na: (be[rr(r, na)], 0, f)),
            pl.BlockSpec((1, 1, tf), lambda f, r, be, na: (be[rr(r, na)], 0, f)),
        ],
        out_specs=pl.BlockSpec((rows, tf), lambda f, r, be, na: (rr(r, na), f)),
        scratch_shapes=[pltpu.VMEM((d, tf), BF16), pltpu.VMEM((d, tf), BF16)],
    )
    return pl.pallas_call(
        _expert_up_body,
        grid_spec=grid_spec,
        out_shape=jax.ShapeDtypeStruct((p, dff), BF16),
        compiler_params=_params(("arbitrary", "arbitrary")),
        name="expert_up",
    )(blk_expert, n_active, xs, w_gate, w_up, b_gate, b_up)


def _expert_down_body(be_ref, na_ref, a_ref, wd_ref, bd_ref, o_ref, wd_s):
    r = pl.program_id(1)
    active = r < na_ref[0]
    first = jnp.logical_or(r == 0, be_ref[r] != be_ref[jnp.maximum(r - 1, 0)])

    @pl.when(jnp.logical_and(active, first))
    def _():
        wd_s[...] = wd_ref[0].astype(BF16)

    @pl.when(active)
    def _():
        o_ref[...] = _dot(a_ref[...], wd_s[...]) + bd_ref[0]


def _expert_down(blk_expert, n_active, a, w_down, b_down, rows, tn):
    p, dff = a.shape
    d = w_down.shape[2]
    nblk = p // rows

    def rr(r, na):
        return jnp.minimum(r, na[0] - 1)

    grid_spec = pltpu.PrefetchScalarGridSpec(
        num_scalar_prefetch=2,
        grid=(d // tn, nblk),
        in_specs=[
            pl.BlockSpec((rows, dff), lambda f, r, be, na: (rr(r, na), 0)),
            pl.BlockSpec((1, dff, tn), lambda f, r, be, na: (be[rr(r, na)], 0, f)),
            pl.BlockSpec((1, 1, tn), lambda f, r, be, na: (be[rr(r, na)], 0, f)),
        ],
        out_specs=pl.BlockSpec((rows, tn), lambda f, r, be, na: (rr(r, na), f)),
        scratch_shapes=[pltpu.VMEM((dff, tn), BF16)],
    )
    return pl.pallas_call(
        _expert_down_body,
        grid_spec=grid_spec,
        out_shape=jax.ShapeDtypeStruct((p, d), F32),
        compiler_params=_params(("arbitrary", "arbitrary")),
        name="expert_down",
    )(blk_expert, n_active, a, w_down, b_down)


def _combine_body(dest_ref, wt_ref, x1_ref, fn_ref, rows_hbm, y_ref, buf, sem):
    tm = x1_ref.shape[0]

    def row_copy(t, kk):
        return pltpu.make_async_copy(rows_hbm.at[pl.ds(dest_ref[0, 0, t * TOP_K + kk], 1)],
                                     buf.at[kk, pl.ds(t, 1)], sem)

    def start(t, carry):
        for kk in range(TOP_K):
            row_copy(t, kk).start()
        return carry
    lax.fori_loop(0, tm, start, 0)

    def wait(t, carry):
        for kk in range(TOP_K):
            row_copy(t, kk).wait()
        return carry
    lax.fori_loop(0, tm, wait, 0)

    wt = wt_ref[...]
    moe = wt[:, 0:1] * buf[0]
    for kk in range(1, TOP_K):
        moe = moe + wt[:, kk:kk + 1] * buf[kk]
    y = x1_ref[...] + moe
    y_ref[...] = y * lax.rsqrt(jnp.mean(y * y, axis=-1, keepdims=True) + EPS) * fn_ref[...]


def _combine(dest, wts, x1, final_norm, out_rows, *, tile0, ntiles, tm):
    d = x1.shape[1]
    return pl.pallas_call(
        _combine_body,
        grid=(ntiles,),
        in_specs=[
            pl.BlockSpec((1, 1, tm * TOP_K), lambda i: (tile0 + i, 0, 0), memory_space=pltpu.SMEM),
            pl.BlockSpec((tm, LANES), lambda i: (tile0 + i, 0)),
            pl.BlockSpec((tm, d), lambda i: (tile0 + i, 0)),
            pl.BlockSpec((1, d), lambda i: (0, 0)),
            pl.BlockSpec(memory_space=pl.ANY),
        ],
        out_specs=pl.BlockSpec((tm, d), lambda i: (i, 0)),
        out_shape=jax.ShapeDtypeStruct((ntiles * tm, d), F32),
        scratch_shapes=[pltpu.VMEM((TOP_K, tm, d), F32), pltpu.SemaphoreType.DMA(())],
        compiler_params=_params(("arbitrary",)),
        name="moe_combine",
    )(dest, wts, x1, final_norm, out_rows)


def _routing_tables(top_idx, rows):
    n = top_idx.shape[0]
    m = n * TOP_K
    e_flat = top_idx.reshape(m)
    onehot = (e_flat[:, None] == jnp.arange(N_EXPERTS, dtype=jnp.int32)[None, :]).astype(jnp.int32)
    csum = jnp.cumsum(onehot, axis=0)
    rank = jnp.sum((csum - onehot) * onehot, axis=1)
    counts = csum[-1]
    padded = (counts + (rows - 1)) // rows * rows
    pends = jnp.cumsum(padded)
    pstarts = pends - padded
    dest = pstarts[e_flat] + rank
    nblk = m // rows + N_EXPERTS
    row_tok = jnp.zeros((nblk * rows,), jnp.int32).at[dest].set(jnp.arange(m, dtype=jnp.int32) // TOP_K)
    blk_expert = jnp.minimum(
        jnp.searchsorted(pends, jnp.arange(nblk, dtype=jnp.int32) * rows, side="right"),
        N_EXPERTS - 1).astype(jnp.int32)
    n_active = (pends[-1:] // rows).astype(jnp.int32)
    return dest.astype(jnp.int32), row_tok.reshape(nblk, 1, rows), blk_expert, n_active


def _pad_rows8(a, at):
    b, r, n = a.shape
    return jnp.zeros((b, SUBLANES, n), a.dtype).at[:, at:at + r].set(a)


def _rope_tables(pos, half):
    inv = jnp.power(ROPE_BASE, -jnp.arange(half, dtype=F32) / half)
    ang = pos.astype(F32)[:, None] * inv[None, :]
    return jnp.cos(ang), jnp.sin(ang)


def kernel(x_prompt, x_sample, state_mlstm_C, state_mlstm_n, state_mlstm_m, state_mlstm_conv, state_ret,
           meta_tokens, norm1, w_in, b_igate, b_fgate, conv_w, conv_b, mlstm_norm, ret_norm,
           w_out_a, w_out_b, w_o, norm2, w_router, b_router, w_gate, b_gate, w_up, b_up,
           w_down, b_down, final_norm):
    bp, tp, d = x_prompt.shape
    bs, ts, _ = x_sample.shape
    depth = norm1.shape[0]
    assert depth == 1, "single-layer stack"
    n_p, n_s = bp * tp, bs * ts
    n_tok = n_p + n_s
    qk_cols = 2 * H_A * _DK
    gate_col0 = qk_cols + H_A * _DV

    w_in0 = w_in[0]
    w_main = jnp.concatenate([w_in0[:, :gate_col0], w_in0[:, gate_col0 + 2 * H_A:]], axis=1).astype(BF16)
    w_if = jnp.zeros((d, LANES), F32).at[:, :2 * H_A].set(w_in0[:, gate_col0:gate_col0 + 2 * H_A])
    w_if_hi, w_if_lo = _split_bf16(w_if)
    b_if = jnp.zeros((1, LANES), F32).at[0, :H_A].set(b_igate[0]).at[0, H_A:2 * H_A].set(b_fgate[0])
    w_r = jnp.zeros((d, LANES), F32).at[:, :N_EXPERTS].set(w_router[0])
    w_r_hi, w_r_lo = _split_bf16(w_r)
    b_r = jnp.zeros((1, LANES), F32).at[0, :N_EXPERTS].set(b_router[0])
    conv_w8 = jnp.zeros((SUBLANES, qk_cols), F32).at[:CONV_W].set(conv_w[0])
    conv_b1 = conv_b[0].reshape(1, qk_cols)
    g1 = norm1[0].reshape(1, d)
    g2 = norm2[0].reshape(1, d)
    gf = final_norm.reshape(1, d)
    mnorm = mlstm_norm[0].reshape(1, H_A * _DV)
    rnorm = ret_norm[0].reshape(1, H_B * _DV)
    log_g = jnp.asarray(np.log1p(-np.power(2.0, -5.0 - np.arange(H_B, dtype=np.float32))).astype(np.float32))

    x_all = jnp.concatenate([x_prompt.reshape(n_p, d), x_sample.reshape(n_s, d)], axis=0)
    tm = next(t for t in (512, 256, 128) if n_tok % t == 0)
    h_all, gates = _norm_gates(x_all, g1, w_if_hi, w_if_lo, b_if, tm)
    z = _in_proj(h_all, w_main, tm, 1024)
    h_meta, gates_meta = _norm_gates(meta_tokens, g1, w_if_hi, w_if_lo, b_if, N_META)
    z_meta = _in_proj(h_meta, w_main, N_META, 1024)

    def gate_rows(g, nb, t):
        return jnp.transpose(g[:, :SUBLANES].reshape(nb, t, SUBLANES), (0, 2, 1))

    zeros_c = jnp.zeros((1, H_A, _DK, _DV), F32)
    zeros_n = jnp.zeros((1, H_A, 1, _DK), F32)
    zeros_m = jnp.zeros((1, H_A, SUBLANES, LANES), F32)
    zeros_conv = jnp.zeros((1, SUBLANES, qk_cols), F32)
    _, c_meta, n_meta, m_meta = _mlstm_scan(
        z_meta, gates_meta, gate_rows(gates_meta, 1, N_META), zeros_conv, conv_w8, conv_b1, mnorm,
        zeros_c, zeros_n, zeros_m, row0=0, gate_row0=0, nb=1, t=N_META, c=N_META,
        share_state=False, ha_prev=None, m_total=N_META)
    conv_meta = _pad_rows8(z_meta[None, N_META - (CONV_W - 1):, :qk_cols], SUBLANES - (CONV_W - 1))
    cp = PROMPT_CHUNK if tp % PROMPT_CHUNK == 0 else tp
    ha, p_c, p_n, p_m = _mlstm_scan(
        z, gates, gate_rows(gates[:n_p], bp, tp), conv_meta, conv_w8, conv_b1, mnorm,
        c_meta, n_meta, m_meta, row0=0, gate_row0=0, nb=bp, t=tp, c=cp,
        share_state=True, ha_prev=None, m_total=n_tok)
    conv_s0 = _pad_rows8(state_mlstm_conv[0], SUBLANES - (CONV_W - 1))
    m_s0 = jnp.broadcast_to(state_mlstm_m[0][:, :, None, None], (bs, H_A, SUBLANES, LANES))
    ha, s_c, s_n, s_m = _mlstm_scan(
        z, gates, gate_rows(gates[n_p:], bs, ts), conv_s0, conv_w8, conv_b1, mnorm,
        state_mlstm_C[0], state_mlstm_n[0].reshape(bs, H_A, 1, _DK), m_s0,
        row0=n_p, gate_row0=n_p, nb=bs, t=ts, c=ts, share_state=False, ha_prev=ha, m_total=n_tok)

    cos_m, sin_m = _rope_tables(jnp.arange(N_META, dtype=jnp.int32), _DK // 2)
    cos_p, sin_p = _rope_tables(N_META + jnp.arange(tp, dtype=jnp.int32), _DK // 2)
    cos_s, sin_s = _rope_tables(N_META + PAST_LEN + jnp.arange(ts, dtype=jnp.int32), _DK // 2)
    zeros_s = jnp.zeros((1, H_B, _DK, _DV), F32)
    _, s_meta = _ret_scan(log_g, z_meta, cos_m, sin_m, rnorm, zeros_s, row0=0, out_row0=0, nb=1,
                          t=N_META, c=N_META, share_state=False, ob_prev=None, m_total=N_META)
    ob, p_ret = _ret_scan(log_g, z, cos_p, sin_p, rnorm, s_meta, row0=0, out_row0=0, nb=bp, t=tp, c=cp,
                          share_state=True, ob_prev=None, m_total=n_tok)
    ob, s_ret = _ret_scan(log_g, z, cos_s, sin_s, rnorm, state_ret[0], row0=n_p, out_row0=n_p, nb=bs,
                          t=ts, c=ts, share_state=False, ob_prev=ob, m_total=n_tok)

    merged = _merge(ha, ob, w_out_a[0].astype(BF16), w_out_b[0].astype(BF16), z, tm, 512)
    x1, h2, top_idx, wts = _wo_router(merged, w_o[0].astype(BF16), x_all, g2, w_r_hi, w_r_lo, b_r, min(tm, 256))

    rows = EXPERT_ROWS
    dest, row_tok, blk_expert, n_active = _routing_tables(top_idx[:, :TOP_K], rows)
    xs = _gather_rows(n_active, row_tok, h2, rows)
    dff = w_gate.shape[3]
    act = _expert_up(blk_expert, n_active, xs, w_gate[0], w_up[0],
                     b_gate[0].reshape(N_EXPERTS, 1, dff), b_up[0].reshape(N_EXPERTS, 1, dff), rows, 1024)
    out_rows = _expert_down(blk_expert, n_active, act, w_down[0], b_down[0].reshape(N_EXPERTS, 1, d), rows, 1024)
    tc = COMBINE_ROWS
    dest3 = dest.reshape(n_tok // tc, 1, tc * TOP_K)
    y_p = _combine(dest3, wts, x1, gf, out_rows, tile0=0, ntiles=n_p // tc, tm=tc)
    y_s = _combine(dest3, wts, x1, gf, out_rows, tile0=n_p // tc, ntiles=n_s // tc, tm=tc)

    def tail_rows(row0, nb, t):
        return jnp.stack([z[row0 + (b + 1) * t - (CONV_W - 1):row0 + (b + 1) * t, :qk_cols] for b in range(nb)])

    p_conv = tail_rows(0, bp, tp)
    s_conv = tail_rows(n_p, bs, ts)
    return (y_p.reshape(bp, tp, d), y_s.reshape(bs, ts, d),
            p_c[None], p_n.reshape(1, bp, H_A, _DK), p_m[None, :, :, 0, 0], p_conv[None], p_ret[None],
            s_c[None], s_n.reshape(1, bs, H_A, _DK), s_m[None, :, :, 0, 0], s_conv[None], s_ret[None])
```

```python
import functools

import numpy as np
import jax
import jax.numpy as jnp
from jax import lax
from jax.experimental import pallas as pl
from jax.experimental.pallas import tpu as pltpu

F32 = jnp.float32
BF16 = jnp.bfloat16

EPS = 1e-5
N_META = 16
PAST_LEN = 1024
H_A = 4
H_B = 8
CONV_W = 4
ROPE_BASE = 10000.0
N_EXPERTS = 32
TOP_K = 4
SWIGLU_LIMIT = 7.0
SWIGLU_ALPHA = 1.702

LANES = 128
SUBLANES = 8
VMEM_LIMIT = 56 * 1024 * 1024
PROMPT_CHUNK = 256
EXPERT_ROWS = 256
DISPATCH_ROWS = 128


def _params(sem, vmem=VMEM_LIMIT):
    return pltpu.CompilerParams(dimension_semantics=sem, vmem_limit_bytes=vmem)


def _dot(a, b):
    return jnp.dot(a, b, preferred_element_type=F32)


def _dot_nt(a, b):
    return lax.dot_general(a, b, (((1,), (1,)), ((), ())), preferred_element_type=F32)


def _dot_tn(a, b):
    return lax.dot_general(a, b, (((0,), (0,)), ((), ())), preferred_element_type=F32)


def _split_bf16(w):
    hi = w.astype(BF16)
    lo = (w - hi.astype(F32)).astype(BF16)
    return hi, lo


def _dot3(a, w_hi, w_lo):
    a_hi = a.astype(BF16)
    a_lo = (a - a_hi.astype(F32)).astype(BF16)
    return _dot(a_hi, w_hi) + _dot(a_hi, w_lo) + _dot(a_lo, w_hi)


def _two_group_specs(tm, d, n_first):
    return [
        pl.BlockSpec((tm, d), lambda i: (jnp.minimum(i, n_first - 1), 0)),
        pl.BlockSpec((tm, d), lambda i: (jnp.maximum(i - n_first, 0), 0)),
    ]


def _norm_gates_body(n_first, xa_ref, xb_ref, g_ref, whi_ref, wlo_ref, b_ref, h_ref, gates_ref):
    def run(x_ref):
        x = x_ref[...]
        y = x * lax.rsqrt(jnp.mean(x * x, axis=-1, keepdims=True) + EPS) * g_ref[...]
        h_ref[...] = y.astype(BF16)
        pre = _dot3(y, whi_ref[...], wlo_ref[...]) + b_ref[...]
        lane = lax.broadcasted_iota(jnp.int32, pre.shape, 1)
        logsig = jnp.minimum(pre, 0.0) - jnp.log1p(jnp.exp(-jnp.abs(pre)))
        gates_ref[...] = jnp.where(lane < H_A, pre, logsig)

    i = pl.program_id(0)
    pl.when(i < n_first)(lambda: run(xa_ref))
    pl.when(i >= n_first)(lambda: run(xb_ref))


def _norm_gates(xa, xb, gain, w_hi, w_lo, bias, tm):
    d = xa.shape[1]
    n_first = xa.shape[0] // tm
    m = xa.shape[0] + (0 if xb is None else xb.shape[0])
    xb = xa if xb is None else xb
    return pl.pallas_call(
        functools.partial(_norm_gates_body, n_first),
        grid=(m // tm,),
        in_specs=_two_group_specs(tm, d, n_first) + [
            pl.BlockSpec((1, d), lambda i: (0, 0)),
            pl.BlockSpec((d, LANES), lambda i: (0, 0)),
            pl.BlockSpec((d, LANES), lambda i: (0, 0)),
            pl.BlockSpec((1, LANES), lambda i: (0, 0)),
        ],
        out_specs=[
            pl.BlockSpec((tm, d), lambda i: (i, 0)),
            pl.BlockSpec((tm, LANES), lambda i: (i, 0)),
        ],
        out_shape=[
            jax.ShapeDtypeStruct((m, d), BF16),
            jax.ShapeDtypeStruct((m, LANES), F32),
        ],
        compiler_params=_params(("parallel",)),
        name="norm_gates",
    )(xa, xb, gain, w_hi, w_lo, bias)


def _mm_body(x_ref, w_ref, o_ref):
    o_ref[...] = _dot(x_ref[...], w_ref[...]).astype(o_ref.dtype)


def _mm_cast_body(x_ref, w_ref, o_ref, w_s):
    @pl.when(pl.program_id(1) == 0)
    def _():
        w_s[...] = w_ref[...].astype(BF16)
    o_ref[...] = _dot(x_ref[...], w_s[...]).astype(o_ref.dtype)


def _in_proj(h, w, tm, tn):
    m, k = h.shape
    n = w.shape[1]
    cast = w.dtype != BF16
    return pl.pallas_call(
        _mm_cast_body if cast else _mm_body,
        grid=(n // tn, m // tm),
        in_specs=[
            pl.BlockSpec((tm, k), lambda j, i: (i, 0)),
            pl.BlockSpec((k, tn), lambda j, i: (0, j)),
        ],
        out_specs=pl.BlockSpec((tm, tn), lambda j, i: (i, j)),
        out_shape=jax.ShapeDtypeStruct((m, n), F32),
        scratch_shapes=[pltpu.VMEM((k, tn), BF16)] if cast else [],
        compiler_params=_params(("parallel", "arbitrary" if cast else "parallel")),
        name="in_proj",
    )(h, w)


def _causal_conv_silu(buf, x_ref, w_ref, b_ref, c):
    buf[SUBLANES:SUBLANES + c, :] = x_ref[...]
    acc = b_ref[...] + w_ref[0:1, :] * buf[5:5 + c, :]
    acc = acc + w_ref[1:2, :] * buf[6:6 + c, :]
    acc = acc + w_ref[2:3, :] * buf[7:7 + c, :]
    acc = acc + w_ref[3:4, :] * buf[8:8 + c, :]
    buf[0:SUBLANES, :] = buf[c:c + SUBLANES, :]
    return acc * jax.nn.sigmoid(acc)


def _mlstm_body(c, dk, nchunks, qpre_ref, kpre_ref, v_ref, oa_ref, gcol_ref, grow_ref,
                cq0_ref, ck0_ref, cwq_ref, cwk_ref, cbq_ref, cbk_ref, mn_ref,
                c0_ref, n0_ref, m0_ref, *rest):
    ha_ref, cout_ref, nout_ref, mout_ref, c_s, n_s, m_s, qbuf, kbuf = rest[-9:]
    h = pl.program_id(1)
    j = pl.program_id(2)

    @pl.when(j == 0)
    def _():
        c_s[...] = c0_ref[0, 0]
        n_s[...] = n0_ref[0, 0]
        m_s[...] = m0_ref[0, 0]
        qbuf[0:SUBLANES, :] = cq0_ref[0]
        kbuf[0:SUBLANES, :] = ck0_ref[0]

    q = _causal_conv_silu(qbuf, qpre_ref, cwq_ref, cbq_ref, c)
    k = _causal_conv_silu(kbuf, kpre_ref, cwk_ref, cbk_ref, c) * (dk ** -0.5)

    gc = gcol_ref[...]
    lane = lax.broadcasted_iota(jnp.int32, gc.shape, 1)
    ig_col = jnp.sum(jnp.where(lane == h, gc, 0.0), axis=1, keepdims=True)
    lf_col = jnp.sum(jnp.where(lane == h + H_A, gc, 0.0), axis=1, keepdims=True)
    gr = grow_ref[0]
    sub = lax.broadcasted_iota(jnp.int32, gr.shape, 0)
    ig_row = jnp.sum(jnp.where(sub == h, gr, 0.0), axis=0, keepdims=True)
    lf_row = jnp.sum(jnp.where(sub == h + H_A, gr, 0.0), axis=0, keepdims=True)

    t_i = lax.broadcasted_iota(jnp.int32, (c, c), 0)
    s_i = lax.broadcasted_iota(jnp.int32, (c, c), 1)
    causal = s_i <= t_i
    b_col = jnp.sum(jnp.where(causal, lf_row, 0.0), axis=1, keepdims=True)
    b_row = jnp.sum(jnp.where(t_i <= s_i, lf_col, 0.0), axis=0, keepdims=True)
    m_prev = m_s[0:1, 0:1]
    log_d = jnp.where(causal, b_col - b_row + ig_row, -jnp.inf)
    inter = b_col + m_prev
    m_tok = jnp.maximum(inter, jnp.max(log_d, axis=1, keepdims=True))
    dw = jnp.exp(log_d - m_tok)
    a_inter = jnp.exp(inter - m_tok)

    qb = q.astype(BF16)
    kb = k.astype(BF16)
    vb = v_ref[...].astype(BF16)
    s = _dot_nt(qb, kb) * dw
    c_old = c_s[...]
    num = a_inter * _dot(qb, c_old.astype(BF16)) + _dot(s.astype(BF16), vb)
    den = a_inter * jnp.sum(q * n_s[...], axis=1, keepdims=True) + jnp.sum(s, axis=1, keepdims=True)
    den = jnp.maximum(jnp.abs(den), jnp.exp(-m_tok))
    hh = num * (1.0 / den)

    b_last = jnp.sum(lf_row, axis=1, keepdims=True)
    logw = b_last - b_col + ig_col
    m_new = jnp.maximum(b_last + m_prev, jnp.max(logw, axis=0, keepdims=True))
    kw = k * jnp.exp(logw - m_new)
    decay = jnp.exp(b_last + m_prev - m_new)
    c_s[...] = decay * c_old + _dot_tn(kw.astype(BF16), vb)
    n_s[...] = decay * n_s[...] + jnp.sum(kw, axis=0, keepdims=True)
    m_s[...] = jnp.broadcast_to(m_new, m_s.shape)

    hn = hh * lax.rsqrt(jnp.mean(hh * hh, axis=-1, keepdims=True) + EPS) * mn_ref[...]
    ha_ref[...] = (hn * jax.nn.sigmoid(oa_ref[...])).astype(BF16)

    @pl.when(j == nchunks - 1)
    def _():
        cout_ref[0, 0] = c_s[...]
        nout_ref[0, 0] = n_s[...]
        mout_ref[0, 0] = m_s[...]


_DK = 256
_DV = 512
_Z1_QK, _Z1_VA = 0, 2048
_Z2_OA, _Z2_QB, _Z2_KB, _Z2_VB, _Z2_GB, _Z2_GA, _Z2_GBM = 0, 2048, 4096, 6144, 10240, 14336, 16384


def _mlstm_scan(z1, z2, gates, grow, conv0, conv_w8, conv_b, mnorm, c0, n0, m0, *, row0, nb, t, c,
                share_state, ha_prev, m_total):
    nchunks = t // c
    zr = row0 // c
    sb = (lambda b: 0) if share_state else (lambda b: b)
    in_specs = [
        pl.BlockSpec((c, _DK), lambda b, h, j: (zr + b * nchunks + j, _Z1_QK // _DK + h)),
        pl.BlockSpec((c, _DK), lambda b, h, j: (zr + b * nchunks + j, _Z1_QK // _DK + H_A + h)),
        pl.BlockSpec((c, _DV), lambda b, h, j: (zr + b * nchunks + j, _Z1_VA // _DV + h)),
        pl.BlockSpec((c, _DV), lambda b, h, j: (zr + b * nchunks + j, _Z2_OA // _DV + h)),
        pl.BlockSpec((c, LANES), lambda b, h, j: (zr + b * nchunks + j, 0)),
        pl.BlockSpec((1, SUBLANES, c), lambda b, h, j: (b, 0, j)),
        pl.BlockSpec((1, SUBLANES, _DK), lambda b, h, j: (sb(b), 0, h)),
        pl.BlockSpec((1, SUBLANES, _DK), lambda b, h, j: (sb(b), 0, H_A + h)),
        pl.BlockSpec((SUBLANES, _DK), lambda b, h, j: (0, h)),
        pl.BlockSpec((SUBLANES, _DK), lambda b, h, j: (0, H_A + h)),
        pl.BlockSpec((1, _DK), lambda b, h, j: (0, h)),
        pl.BlockSpec((1, _DK), lambda b, h, j: (0, H_A + h)),
        pl.BlockSpec((1, _DV), lambda b, h, j: (0, h)),
        pl.BlockSpec((1, 1, _DK, _DV), lambda b, h, j: (sb(b), h, 0, 0)),
        pl.BlockSpec((1, 1, 1, _DK), lambda b, h, j: (sb(b), h, 0, 0)),
        pl.BlockSpec((1, 1, SUBLANES, LANES), lambda b, h, j: (sb(b), h, 0, 0)),
    ]
    args = [z1, z1, z1, z2, gates, grow, conv0, conv0, conv_w8, conv_w8, conv_b, conv_b, mnorm, c0, n0, m0]
    aliases = {}
    if ha_prev is not None:
        in_specs.append(pl.BlockSpec(memory_space=pl.ANY))
        args.append(ha_prev)
        aliases = {len(args) - 1: 0}
    out_specs = [
        pl.BlockSpec((c, _DV), lambda b, h, j: (zr + b * nchunks + j, h)),
        pl.BlockSpec((1, 1, _DK, _DV), lambda b, h, j: (b, h, 0, 0)),
        pl.BlockSpec((1, 1, 1, _DK), lambda b, h, j: (b, h, 0, 0)),
        pl.BlockSpec((1, 1, SUBLANES, LANES), lambda b, h, j: (b, h, 0, 0)),
    ]
    out_shape = [
        jax.ShapeDtypeStruct((m_total, H_A * _DV), BF16),
        jax.ShapeDtypeStruct((nb, H_A, _DK, _DV), F32),
        jax.ShapeDtypeStruct((nb, H_A, 1, _DK), F32),
        jax.ShapeDtypeStruct((nb, H_A, SUBLANES, LANES), F32),
    ]
    return pl.pallas_call(
        functools.partial(_mlstm_body, c, _DK, nchunks),
        grid=(nb, H_A, nchunks),
        in_specs=in_specs,
        out_specs=out_specs,
        out_shape=out_shape,
        scratch_shapes=[
            pltpu.VMEM((_DK, _DV), F32),
            pltpu.VMEM((1, _DK), F32),
            pltpu.VMEM((SUBLANES, LANES), F32),
            pltpu.VMEM((c + SUBLANES, _DK), F32),
            pltpu.VMEM((c + SUBLANES, _DK), F32),
        ],
        input_output_aliases=aliases,
        compiler_params=_params(("parallel", "parallel", "arbitrary")),
        name="mlstm_scan",
    )(*args)


def _rotary(x, cos, sin):
    half = x.shape[-1] // 2
    x1 = x[:, :half]
    x2 = x[:, half:]
    return jnp.concatenate([x1 * cos - x2 * sin, x2 * cos + x1 * sin], axis=-1)


def _ret_body(c, dk, nchunks, lg_ref, q_ref, k_ref, v_ref, g_ref, cos_ref, sin_ref, rn_ref, s0_ref, *rest):
    ob_ref, sout_ref, s_s = rest[-3:]
    h = pl.program_id(1)
    j = pl.program_id(2)

    @pl.when(j == 0)
    def _():
        s_s[...] = s0_ref[0, 0]

    lg = lg_ref[h]
    cos = cos_ref[...]
    sin = sin_ref[...]
    qb = _rotary(q_ref[...], cos, sin).astype(BF16)
    kr = _rotary(k_ref[...], cos, sin) * (dk ** -0.5)
    vb = v_ref[...].astype(BF16)

    t_i = lax.broadcasted_iota(jnp.int32, (c, c), 0)
    s_i = lax.broadcasted_iota(jnp.int32, (c, c), 1)
    causal = t_i >= s_i
    diff = jnp.where(causal, t_i - s_i, 0).astype(F32)
    dmat = jnp.where(causal, jnp.exp(lg * diff), 0.0)
    t_col = lax.broadcasted_iota(jnp.int32, (c, 1), 0).astype(F32)

    scores = _dot_nt(qb, kr.astype(BF16)) * dmat
    s_old = s_s[...]
    cross = jnp.exp(lg * (t_col + 1.0))
    o = _dot(scores.astype(BF16), vb) + cross * _dot(qb, s_old.astype(BF16))
    wk = jnp.exp(lg * (c - 1.0 - t_col))
    s_s[...] = jnp.exp(lg * c) * s_old + _dot_tn((kr * wk).astype(BF16), vb)

    mu = jnp.mean(o, axis=-1, keepdims=True)
    oc = o - mu
    var = jnp.mean(oc * oc, axis=-1, keepdims=True)
    g = g_ref[...]
    ob_ref[...] = (oc * lax.rsqrt(var + EPS) * rn_ref[...] * (g * jax.nn.sigmoid(g))).astype(BF16)

    @pl.when(j == nchunks - 1)
    def _():
        sout_ref[0, 0] = s_s[...]


def _ret_scan(log_g, z2, cos, sin, rnorm, s0, *, row0, nb, t, c, share_state, ob_prev, m_total):
    nchunks = t // c
    zr = row0 // c
    sb = (lambda b: 0) if share_state else (lambda b: b)
    in_specs = [
        pl.BlockSpec((c, _DK), lambda b, h, j, lg: (zr + b * nchunks + j, _Z2_QB // _DK + h)),
        pl.BlockSpec((c, _DK), lambda b, h, j, lg: (zr + b * nchunks + j, _Z2_KB // _DK + h)),
        pl.BlockSpec((c, _DV), lambda b, h, j, lg: (zr + b * nchunks + j, _Z2_VB // _DV + h)),
        pl.BlockSpec((c, _DV), lambda b, h, j, lg: (zr + b * nchunks + j, _Z2_GB // _DV + h)),
        pl.BlockSpec((c, _DK // 2), lambda b, h, j, lg: (j, 0)),
        pl.BlockSpec((c, _DK // 2), lambda b, h, j, lg: (j, 0)),
        pl.BlockSpec((1, _DV), lambda b, h, j, lg: (0, h)),
        pl.BlockSpec((1, 1, _DK, _DV), lambda b, h, j, lg: (sb(b), h, 0, 0)),
    ]
    args = [z2, z2, z2, z2, cos, sin, rnorm, s0]
    aliases = {}
    if ob_prev is not None:
        in_specs.append(pl.BlockSpec(memory_space=pl.ANY))
        args.append(ob_prev)
        aliases = {len(args): 0}
    grid_spec = pltpu.PrefetchScalarGridSpec(
        num_scalar_prefetch=1,
        grid=(nb, H_B, nchunks),
        in_specs=in_specs,
        out_specs=[
            pl.BlockSpec((c, _DV), lambda b, h, j, lg: (zr + b * nchunks + j, h)),
            pl.BlockSpec((1, 1, _DK, _DV), lambda b, h, j, lg: (b, h, 0, 0)),
        ],
        scratch_shapes=[pltpu.VMEM((_DK, _DV), F32)],
    )
    return pl.pallas_call(
        functools.partial(_ret_body, c, _DK, nchunks),
        grid_spec=grid_spec,
        out_shape=[
            jax.ShapeDtypeStruct((m_total, H_B * _DV), BF16),
            jax.ShapeDtypeStruct((nb, H_B, _DK, _DV), F32),
        ],
        input_output_aliases=aliases,
        compiler_params=_params(("parallel", "parallel", "arbitrary")),
        name="ret_scan",
    )(log_g, *args)


def _merge_body(ha_ref, ob_ref, wa_ref, wb_ref, ga_ref, gb_ref, o_ref):
    ya = _dot(ha_ref[...], wa_ref[...])
    yb = _dot(ob_ref[...], wb_ref[...])
    o_ref[...] = (jax.nn.sigmoid(ga_ref[...]) * ya + jax.nn.sigmoid(gb_ref[...]) * yb).astype(o_ref.dtype)


def _merge(ha, ob, wa, wb, z2, tm, tn):
    m = ha.shape[0]
    d = wa.shape[1]
    return pl.pallas_call(
        _merge_body,
        grid=(d // tn, m // tm),
        in_specs=[
            pl.BlockSpec((tm, ha.shape[1]), lambda j, i: (i, 0)),
            pl.BlockSpec((tm, ob.shape[1]), lambda j, i: (i, 0)),
            pl.BlockSpec((wa.shape[0], tn), lambda j, i: (0, j)),
            pl.BlockSpec((wb.shape[0], tn), lambda j, i: (0, j)),
            pl.BlockSpec((tm, tn), lambda j, i: (i, _Z2_GA // tn + j)),
            pl.BlockSpec((tm, tn), lambda j, i: (i, _Z2_GBM // tn + j)),
        ],
        out_specs=pl.BlockSpec((tm, tn), lambda j, i: (i, j)),
        out_shape=jax.ShapeDtypeStruct((m, d), BF16),
        compiler_params=_params(("parallel", "parallel")),
        name="merge",
    )(ha, ob, wa, wb, z2, z2)


def _wo_router_body(n_first, mg_ref, wo_ref, xa_ref, xb_ref, g2_ref, wrh_ref, wrl_ref, br_ref,
                    x1_ref, h2_ref, idx_ref, wt_ref):
    def run(x_ref):
        x1 = x_ref[...] + _dot(mg_ref[...], wo_ref[...])
        x1_ref[...] = x1
        h2 = x1 * lax.rsqrt(jnp.mean(x1 * x1, axis=-1, keepdims=True) + EPS) * g2_ref[...]
        h2_ref[...] = h2
        logits = _dot3(h2, wrh_ref[...], wrl_ref[...]) + br_ref[...]
        lane = lax.broadcasted_iota(jnp.int32, logits.shape, 1)
        cur = jnp.where(lane < N_EXPERTS, logits, -jnp.inf)
        vals, idxs = [], []
        for _ in range(TOP_K):
            mx = jnp.max(cur, axis=1, keepdims=True)
            ix = jnp.min(jnp.where(cur == mx, lane, LANES), axis=1, keepdims=True)
            vals.append(mx)
            idxs.append(ix)
            cur = jnp.where(lane == ix, -jnp.inf, cur)
        exps = [jnp.exp(v - vals[0]) for v in vals]
        tot = exps[0] + exps[1] + exps[2] + exps[3]
        idx_out = jnp.zeros(logits.shape, jnp.int32)
        wt_out = jnp.zeros(logits.shape, F32)
        for kk in range(TOP_K):
            idx_out = jnp.where(lane == kk, idxs[kk], idx_out)
            wt_out = jnp.where(lane == kk, exps[kk] / tot, wt_out)
        idx_ref[...] = idx_out
        wt_ref[...] = wt_out

    i = pl.program_id(0)
    pl.when(i < n_first)(lambda: run(xa_ref))
    pl.when(i >= n_first)(lambda: run(xb_ref))


def _wo_router(merged, wo, xa, xb, g2, wr_hi, wr_lo, br, tm):
    m, d = merged.shape
    n_first = xa.shape[0] // tm
    return pl.pallas_call(
        functools.partial(_wo_router_body, n_first),
        grid=(m // tm,),
        in_specs=[
            pl.BlockSpec((tm, d), lambda i: (i, 0)),
            pl.BlockSpec((d, d), lambda i: (0, 0)),
        ] + _two_group_specs(tm, d, n_first) + [
            pl.BlockSpec((1, d), lambda i: (0, 0)),
            pl.BlockSpec((d, LANES), lambda i: (0, 0)),
            pl.BlockSpec((d, LANES), lambda i: (0, 0)),
            pl.BlockSpec((1, LANES), lambda i: (0, 0)),
        ],
        out_specs=[
            pl.BlockSpec((tm, d), lambda i: (i, 0)),
            pl.BlockSpec((tm, d), lambda i: (i, 0)),
            pl.BlockSpec((tm, LANES), lambda i: (i, 0)),
            pl.BlockSpec((tm, LANES), lambda i: (i, 0)),
        ],
        out_shape=[
            jax.ShapeDtypeStruct((m, d), F32),
            jax.ShapeDtypeStruct((m, d), F32),
            jax.ShapeDtypeStruct((m, LANES), jnp.int32),
            jax.ShapeDtypeStruct((m, LANES), F32),
        ],
        compiler_params=_params(("parallel",)),
        name="wo_router",
    )(merged, wo, xa, xb, g2, wr_hi, wr_lo, br)


def _dispatch_body(dest_ref, h_ref, xs_hbm, sem):
    tm = h_ref.shape[0]

    def row_copy(t, kk):
        return pltpu.make_async_copy(h_ref.at[pl.ds(t, 1)],
                                     xs_hbm.at[pl.ds(dest_ref[0, 0, t * TOP_K + kk], 1)], sem)

    def start(t, carry):
        for kk in range(TOP_K):
            row_copy(t, kk).start()
        return carry
    lax.fori_loop(0, tm, start, 0)

    def wait(t, carry):
        for kk in range(TOP_K):
            row_copy(t, kk).wait()
        return carry
    lax.fori_loop(0, tm, wait, 0)


def _dispatch(dest3, h2, tm):
    n, d = h2.shape
    return pl.pallas_call(
        _dispatch_body,
        grid=(n // tm,),
        in_specs=[
            pl.BlockSpec((1, 1, tm * TOP_K), lambda i: (i, 0, 0), memory_space=pltpu.SMEM),
            pl.BlockSpec((tm, d), lambda i: (i, 0)),
        ],
        out_specs=pl.BlockSpec(memory_space=pl.ANY),
        out_shape=jax.ShapeDtypeStruct((n * TOP_K, d), h2.dtype),
        scratch_shapes=[pltpu.SemaphoreType.DMA(())],
        compiler_params=_params(("arbitrary",)),
        name="moe_dispatch",
    )(dest3, h2)


def _item_flags(t, ie_ref, fb_ref, ni_ref):
    active = t < ni_ref[0]
    new_expert = jnp.logical_or(t == 0, ie_ref[t] != ie_ref[jnp.maximum(t - 1, 0)])
    return active, jnp.logical_and(active, new_expert), fb_ref[t] == 1


def _row_mask(rows, t, lo_ref, hi_ref):
    row = lax.broadcasted_iota(jnp.int32, (rows, 1), 0)
    return jnp.logical_and(row >= lo_ref[t], row < hi_ref[t])


def _expert_up_body(ib_ref, ie_ref, lo_ref, hi_ref, fb_ref, ni_ref,
                    x_ref, wg_ref, wu_ref, bg_ref, bu_ref, a_ref, wg_s, wu_s):
    t = pl.program_id(1)
    active, new_expert, first_visit = _item_flags(t, ie_ref, fb_ref, ni_ref)

    @pl.when(new_expert)
    def _():
        wg_s[...] = wg_ref[0].astype(BF16)
        wu_s[...] = wu_ref[0].astype(BF16)

    def compute():
        x = x_ref[...].astype(BF16)
        g = jnp.minimum(_dot(x, wg_s[...]) + bg_ref[0], SWIGLU_LIMIT)
        u = jnp.clip(_dot(x, wu_s[...]) + bu_ref[0], -SWIGLU_LIMIT, SWIGLU_LIMIT)
        return (g * jax.nn.sigmoid(SWIGLU_ALPHA * g) * (u + 1.0)).astype(BF16)

    mask = _row_mask(x_ref.shape[0], t, lo_ref, hi_ref)

    @pl.when(jnp.logical_and(active, first_visit))
    def _():
        a_ref[...] = jnp.where(mask, compute(), jnp.zeros(a_ref.shape, BF16))

    @pl.when(jnp.logical_and(active, jnp.logical_not(first_visit)))
    def _():
        a_ref[...] = jnp.where(mask, compute(), a_ref[...])


def _item_spec(shape, fn):
    def index_map(f, t, ib, ie, lo, hi, fb, ni):
        tc = jnp.minimum(t, ni[0] - 1)
        return fn(tc, f, ib, ie)
    return pl.BlockSpec(shape, index_map)


def _expert_up(tables, xs, w_gate, w_up, b_gate, b_up, rows, tf):
    p, d = xs.shape
    dff = w_gate.shape[2]
    n_items = tables[0].shape[0]
    grid_spec = pltpu.PrefetchScalarGridSpec(
        num_scalar_prefetch=6,
        grid=(dff // tf, n_items),
        in_specs=[
            _item_spec((rows, d), lambda t, f, ib, ie: (ib[t], 0)),
            _item_spec((1, d, tf), lambda t, f, ib, ie: (ie[t], 0, f)),
            _item_spec((1, d, tf), lambda t, f, ib, ie: (ie[t], 0, f)),
            _item_spec((1, 1, tf), lambda t, f, ib, ie: (ie[t], 0, f)),
            _item_spec((1, 1, tf), lambda t, f, ib, ie: (ie[t], 0, f)),
        ],
        out_specs=_item_spec((rows, tf), lambda t, f, ib, ie: (ib[t], f)),
        scratch_shapes=[pltpu.VMEM((d, tf), BF16), pltpu.VMEM((d, tf), BF16)],
    )
    return pl.pallas_call(
        _expert_up_body,
        grid_spec=grid_spec,
        out_shape=jax.ShapeDtypeStruct((p, dff), BF16),
        compiler_params=_params(("arbitrary", "arbitrary")),
        name="expert_up",
    )(*tables, xs, w_gate, w_up, b_gate, b_up)


def _expert_down_body(ib_ref, ie_ref, lo_ref, hi_ref, fb_ref, ni_ref, a_ref, wd_ref, bd_ref, o_ref, wd_s):
    t = pl.program_id(1)
    active, new_expert, first_visit = _item_flags(t, ie_ref, fb_ref, ni_ref)

    @pl.when(new_expert)
    def _():
        wd_s[...] = wd_ref[0].astype(BF16)

    def compute():
        return _dot(a_ref[...], wd_s[...]) + bd_ref[0]

    mask = _row_mask(a_ref.shape[0], t, lo_ref, hi_ref)

    @pl.when(jnp.logical_and(active, first_visit))
    def _():
        o_ref[...] = jnp.where(mask, compute(), 0.0)

    @pl.when(jnp.logical_and(active, jnp.logical_not(first_visit)))
    def _():
        o_ref[...] = jnp.where(mask, compute(), o_ref[...])


def _expert_down(tables, a, w_down, b_down, rows, tn):
    p, dff = a.shape
    d = w_down.shape[2]
    n_items = tables[0].shape[0]
    grid_spec = pltpu.PrefetchScalarGridSpec(
        num_scalar_prefetch=6,
        grid=(d // tn, n_items),
        in_specs=[
            _item_spec((rows, dff), lambda t, f, ib, ie: (ib[t], 0)),
            _item_spec((1, dff, tn), lambda t, f, ib, ie: (ie[t], 0, f)),
            _item_spec((1, 1, tn), lambda t, f, ib, ie: (ie[t], 0, f)),
        ],
        out_specs=_item_spec((rows, tn), lambda t, f, ib, ie: (ib[t], f)),
        scratch_shapes=[pltpu.VMEM((dff, tn), BF16)],
    )
    return pl.pallas_call(
        _expert_down_body,
        grid_spec=grid_spec,
        out_shape=jax.ShapeDtypeStruct((p, d), F32),
        compiler_params=_params(("arbitrary", "arbitrary")),
        name="expert_down",
    )(*tables, a, w_down, b_down)


def _combine_body(dest_ref, wt_ref, x1_ref, fn_ref, rows_hbm, y_ref, buf, sem):
    tm = x1_ref.shape[0]

    def row_copy(t, kk):
        return pltpu.make_async_copy(rows_hbm.at[pl.ds(dest_ref[0, 0, t * TOP_K + kk], 1)],
                                     buf.at[kk, pl.ds(t, 1)], sem)

    def start(t, carry):
        for kk in range(TOP_K):
            row_copy(t, kk).start()
        return carry
    lax.fori_loop(0, tm, start, 0)

    def wait(t, carry):
        for kk in range(TOP_K):
            row_copy(t, kk).wait()
        return carry
    lax.fori_loop(0, tm, wait, 0)

    wt = wt_ref[...]
    moe = wt[:, 0:1] * buf[0]
    for kk in range(1, TOP_K):
        moe = moe + wt[:, kk:kk + 1] * buf[kk]
    y = x1_ref[...] + moe
    y_ref[...] = y * lax.rsqrt(jnp.mean(y * y, axis=-1, keepdims=True) + EPS) * fn_ref[...]


def _combine(dest3, wts, x1, final_norm, out_rows, *, tile0, ntiles, tm):
    d = x1.shape[1]
    return pl.pallas_call(
        _combine_body,
        grid=(ntiles,),
        in_specs=[
            pl.BlockSpec((1, 1, tm * TOP_K), lambda i: (tile0 + i, 0, 0), memory_space=pltpu.SMEM),
            pl.BlockSpec((tm, LANES), lambda i: (tile0 + i, 0)),
            pl.BlockSpec((tm, d), lambda i: (tile0 + i, 0)),
            pl.BlockSpec((1, d), lambda i: (0, 0)),
            pl.BlockSpec(memory_space=pl.ANY),
        ],
        out_specs=pl.BlockSpec((tm, d), lambda i: (i, 0)),
        out_shape=jax.ShapeDtypeStruct((ntiles * tm, d), F32),
        scratch_shapes=[pltpu.VMEM((TOP_K, tm, d), F32), pltpu.SemaphoreType.DMA(())],
        compiler_params=_params(("arbitrary",)),
        name="moe_combine",
    )(dest3, wts, x1, final_norm, out_rows)


def _routing_tables(top_idx, rows):
    n = top_idx.shape[0]
    m = n * TOP_K
    experts = jnp.arange(N_EXPERTS, dtype=jnp.int32)
    e_flat = top_idx.reshape(m)
    onehot = (e_flat[:, None] == experts[None, :]).astype(jnp.int32)
    csum = jnp.cumsum(onehot, axis=0)
    counts = csum[-1]
    ends = jnp.cumsum(counts)
    starts = ends - counts
    dest = jnp.sum(onehot * (csum - 1 + starts[None, :]), axis=1)

    first_blk = starts // rows
    last_blk = (ends - 1) // rows
    n_it = jnp.where(counts > 0, last_blk - first_blk + 1, 0)
    it_end = jnp.cumsum(n_it)
    it_start = it_end - n_it
    n_items = it_end[-1:]
    max_items = m // rows + N_EXPERTS
    t = jnp.minimum(jnp.arange(max_items, dtype=jnp.int32), n_items[0] - 1)
    ie = jnp.sum((it_end[None, :] <= t[:, None]).astype(jnp.int32), axis=1)
    sel = (ie[:, None] == experts[None, :]).astype(jnp.int32)
    pick = lambda v: jnp.sum(sel * v[None, :], axis=1)
    ib = pick(first_blk) + t - pick(it_start)
    lo = jnp.clip(pick(starts) - ib * rows, 0, rows)
    hi = jnp.clip(pick(ends) - ib * rows, 0, rows)
    fb = jnp.concatenate([jnp.ones((1,), jnp.int32), (ib[1:] != ib[:-1]).astype(jnp.int32)])
    i32 = lambda v: v.astype(jnp.int32)
    return i32(dest), (i32(ib), i32(ie), i32(lo), i32(hi), fb, i32(n_items))


def _pad_rows8(a, at):
    b, r, n = a.shape
    return jnp.zeros((b, SUBLANES, n), a.dtype).at[:, at:at + r].set(a)


def _rope_tables(pos, half):
    inv = jnp.power(ROPE_BASE, -jnp.arange(half, dtype=F32) / half)
    ang = pos.astype(F32)[:, None] * inv[None, :]
    return jnp.cos(ang), jnp.sin(ang)


def kernel(x_prompt, x_sample, state_mlstm_C, state_mlstm_n, state_mlstm_m, state_mlstm_conv, state_ret,
           meta_tokens, norm1, w_in, b_igate, b_fgate, conv_w, conv_b, mlstm_norm, ret_norm,
           w_out_a, w_out_b, w_o, norm2, w_router, b_router, w_gate, b_gate, w_up, b_up,
           w_down, b_down, final_norm):
    bp, tp, d = x_prompt.shape
    bs, ts, _ = x_sample.shape
    assert norm1.shape[0] == 1, "single-layer stack"
    n_p, n_s = bp * tp, bs * ts
    n_tok = n_p + n_s
    qk_cols = 2 * H_A * _DK
    gate_col0 = qk_cols + H_A * _DV

    w_in0 = w_in[0]
    w_first = w_in0[:, :gate_col0]
    w_rest = w_in0[:, gate_col0 + 2 * H_A:].astype(BF16)
    w_if = jnp.zeros((d, LANES), F32).at[:, :2 * H_A].set(w_in0[:, gate_col0:gate_col0 + 2 * H_A])
    w_if_hi, w_if_lo = _split_bf16(w_if)
    b_if = jnp.zeros((1, LANES), F32).at[0, :H_A].set(b_igate[0]).at[0, H_A:2 * H_A].set(b_fgate[0])
    w_r = jnp.zeros((d, LANES), F32).at[:, :N_EXPERTS].set(w_router[0])
    w_r_hi, w_r_lo = _split_bf16(w_r)
    b_r = jnp.zeros((1, LANES), F32).at[0, :N_EXPERTS].set(b_router[0])
    conv_w8 = jnp.zeros((SUBLANES, qk_cols), F32).at[:CONV_W].set(conv_w[0])
    conv_b1 = conv_b[0].reshape(1, qk_cols)
    g1 = norm1[0].reshape(1, d)
    g2 = norm2[0].reshape(1, d)
    gf = final_norm.reshape(1, d)
    mnorm = mlstm_norm[0].reshape(1, H_A * _DV)
    rnorm = ret_norm[0].reshape(1, H_B * _DV)
    log_g = jnp.asarray(np.log1p(-np.power(2.0, -5.0 - np.arange(H_B, dtype=np.float32))).astype(np.float32))

    xp2 = x_prompt.reshape(n_p, d)
    xs2 = x_sample.reshape(n_s, d)
    tm = next(t for t in (512, 256, 128) if n_p % t == 0 and n_s % t == 0)
    h_all, gates = _norm_gates(xp2, xs2, g1, w_if_hi, w_if_lo, b_if, tm)
    z1 = _in_proj(h_all, w_first, tm, 1024)
    z2 = _in_proj(h_all, w_rest, tm, 1024)
    h_meta, gates_meta = _norm_gates(meta_tokens, None, g1, w_if_hi, w_if_lo, b_if, N_META)
    z1_meta = _in_proj(h_meta, w_first, N_META, 1024)
    z2_meta = _in_proj(h_meta, w_rest, N_META, 1024)

    def gate_rows(g, nb, t):
        return jnp.transpose(g[:, :SUBLANES].reshape(nb, t, SUBLANES), (0, 2, 1))

    zeros_c = jnp.zeros((1, H_A, _DK, _DV), F32)
    zeros_n = jnp.zeros((1, H_A, 1, _DK), F32)
    zeros_m = jnp.zeros((1, H_A, SUBLANES, LANES), F32)
    zeros_conv = jnp.zeros((1, SUBLANES, qk_cols), F32)
    _, c_meta, n_meta, m_meta = _mlstm_scan(
        z1_meta, z2_meta, gates_meta, gate_rows(gates_meta, 1, N_META), zeros_conv, conv_w8, conv_b1, mnorm,
        zeros_c, zeros_n, zeros_m, row0=0, nb=1, t=N_META, c=N_META,
        share_state=False, ha_prev=None, m_total=N_META)
    conv_meta = _pad_rows8(z1_meta[None, N_META - (CONV_W - 1):, :qk_cols], SUBLANES - (CONV_W - 1))
    cp = PROMPT_CHUNK if tp % PROMPT_CHUNK == 0 else tp
    ha, p_c, p_n, p_m = _mlstm_scan(
        z1, z2, gates, gate_rows(gates[:n_p], bp, tp), conv_meta, conv_w8, conv_b1, mnorm,
        c_meta, n_meta, m_meta, row0=0, nb=bp, t=tp, c=cp,
        share_state=True, ha_prev=None, m_total=n_tok)
    conv_s0 = _pad_rows8(state_mlstm_conv[0], SUBLANES - (CONV_W - 1))
    m_s0 = jnp.broadcast_to(state_mlstm_m[0][:, :, None, None], (bs, H_A, SUBLANES, LANES))
    ha, s_c, s_n, s_m = _mlstm_scan(
        z1, z2, gates, gate_rows(gates[n_p:], bs, ts), conv_s0, conv_w8, conv_b1, mnorm,
        state_mlstm_C[0], state_mlstm_n[0].reshape(bs, H_A, 1, _DK), m_s0,
        row0=n_p, nb=bs, t=ts, c=ts, share_state=False, ha_prev=ha, m_total=n_tok)

    cos_m, sin_m = _rope_tables(jnp.arange(N_META, dtype=jnp.int32), _DK // 2)
    cos_p, sin_p = _rope_tables(N_META + jnp.arange(tp, dtype=jnp.int32), _DK // 2)
    cos_s, sin_s = _rope_tables(N_META + PAST_LEN + jnp.arange(ts, dtype=jnp.int32), _DK // 2)
    zeros_s = jnp.zeros((1, H_B, _DK, _DV), F32)
    _, s_meta = _ret_scan(log_g, z2_meta, cos_m, sin_m, rnorm, zeros_s, row0=0, nb=1,
                          t=N_META, c=N_META, share_state=False, ob_prev=None, m_total=N_META)
    ob, p_ret = _ret_scan(log_g, z2, cos_p, sin_p, rnorm, s_meta, row0=0, nb=bp, t=tp, c=cp,
                          share_state=True, ob_prev=None, m_total=n_tok)
    ob, s_ret = _ret_scan(log_g, z2, cos_s, sin_s, rnorm, state_ret[0], row0=n_p, nb=bs,
                          t=ts, c=ts, share_state=False, ob_prev=ob, m_total=n_tok)

    merged = _merge(ha, ob, w_out_a[0].astype(BF16), w_out_b[0].astype(BF16), z2, tm, 512)
    x1, h2, top_idx, wts = _wo_router(merged, w_o[0].astype(BF16), xp2, xs2, g2, w_r_hi, w_r_lo, b_r,
                                      min(tm, 256))

    rows = EXPERT_ROWS
    dest, tables = _routing_tables(top_idx[:, :TOP_K], rows)
    td = DISPATCH_ROWS
    dest3 = dest.reshape(n_tok // td, 1, td * TOP_K)
    xs = _dispatch(dest3, h2, td)
    dff = w_gate.shape[3]
    act = _expert_up(tables, xs, w_gate[0], w_up[0],
                     b_gate[0].reshape(N_EXPERTS, 1, dff), b_up[0].reshape(N_EXPERTS, 1, dff), rows, 1024)
    out_rows = _expert_down(tables, act, w_down[0], b_down[0].reshape(N_EXPERTS, 1, d), rows, 1024)
    y_p = _combine(dest3, wts, x1, gf, out_rows, tile0=0, ntiles=n_p // td, tm=td)
    y_s = _combine(dest3, wts, x1, gf, out_rows, tile0=n_p // td, ntiles=n_s // td, tm=td)

    def tail_rows(row0, nb, t):
        return jnp.stack([z1[row0 + (b + 1) * t - (CONV_W - 1):row0 + (b + 1) * t, :qk_cols] for b in range(nb)])

    p_conv = tail_rows(0, bp, tp)
    s_conv = tail_rows(n_p, bs, ts)
    return (y_p.reshape(bp, tp, d), y_s.reshape(bs, ts, d),
            p_c[None], p_n.reshape(1, bp, H_A, _DK), p_m[None, :, :, 0, 0], p_conv[None], p_ret[None],
            s_c[None], s_n.reshape(1, bs, H_A, _DK), s_m[None, :, :, 0, 0], s_conv[None], s_ret[None])
```

```python
import functools

import numpy as np
import jax
import jax.numpy as jnp
from jax import lax
from jax.experimental import pallas as pl
from jax.experimental.pallas import tpu as pltpu

F32 = jnp.float32
BF16 = jnp.bfloat16

EPS = 1e-5
N_META = 16
PAST_LEN = 1024
H_A = 4
H_B = 8
CONV_W = 4
ROPE_BASE = 10000.0
N_EXPERTS = 32
TOP_K = 4
SWIGLU_LIMIT = 7.0
SWIGLU_ALPHA = 1.702

LANES = 128
SUBLANES = 8
VMEM_LIMIT = 56 * 1024 * 1024
PROMPT_CHUNK = 256
EXPERT_ROWS = 256
DISPATCH_ROWS = 128


def _params(sem, vmem=VMEM_LIMIT):
    return pltpu.CompilerParams(dimension_semantics=sem, vmem_limit_bytes=vmem)


def _dot(a, b):
    return jnp.dot(a, b, preferred_element_type=F32)


def _dot_nt(a, b):
    return lax.dot_general(a, b, (((1,), (1,)), ((), ())), preferred_element_type=F32)


def _dot_tn(a, b):
    return lax.dot_general(a, b, (((0,), (0,)), ((), ())), preferred_element_type=F32)


def _split_bf16(w):
    hi = w.astype(BF16)
    lo = (w - hi.astype(F32)).astype(BF16)
    return hi, lo


def _dot3(a, w_hi, w_lo):
    a_hi = a.astype(BF16)
    a_lo = (a - a_hi.astype(F32)).astype(BF16)
    return _dot(a_hi, w_hi) + _dot(a_hi, w_lo) + _dot(a_lo, w_hi)


def _two_group_specs(tm, d, n_first):
    return [
        pl.BlockSpec((tm, d), lambda i: (jnp.minimum(i, n_first - 1), 0)),
        pl.BlockSpec((tm, d), lambda i: (jnp.maximum(i - n_first, 0), 0)),
    ]


def _norm_gates_body(n_first, xa_ref, xb_ref, g_ref, whi_ref, wlo_ref, b_ref, h_ref, gates_ref):
    def run(x_ref):
        x = x_ref[...]
        y = x * lax.rsqrt(jnp.mean(x * x, axis=-1, keepdims=True) + EPS) * g_ref[...]
        h_ref[...] = y.astype(BF16)
        pre = _dot3(y, whi_ref[...], wlo_ref[...]) + b_ref[...]
        lane = lax.broadcasted_iota(jnp.int32, pre.shape, 1)
        logsig = jnp.minimum(pre, 0.0) - jnp.log1p(jnp.exp(-jnp.abs(pre)))
        gates_ref[...] = jnp.where(lane < H_A, pre, logsig)

    i = pl.program_id(0)
    pl.when(i < n_first)(lambda: run(xa_ref))
    pl.when(i >= n_first)(lambda: run(xb_ref))


def _norm_gates(xa, xb, gain, w_hi, w_lo, bias, tm):
    d = xa.shape[1]
    n_first = xa.shape[0] // tm
    m = xa.shape[0] + (0 if xb is None else xb.shape[0])
    xb = xa if xb is None else xb
    return pl.pallas_call(
        functools.partial(_norm_gates_body, n_first),
        grid=(m // tm,),
        in_specs=_two_group_specs(tm, d, n_first) + [
            pl.BlockSpec((1, d), lambda i: (0, 0)),
            pl.BlockSpec((d, LANES), lambda i: (0, 0)),
            pl.BlockSpec((d, LANES), lambda i: (0, 0)),
            pl.BlockSpec((1, LANES), lambda i: (0, 0)),
        ],
        out_specs=[
            pl.BlockSpec((tm, d), lambda i: (i, 0)),
            pl.BlockSpec((tm, LANES), lambda i: (i, 0)),
        ],
        out_shape=[
            jax.ShapeDtypeStruct((m, d), BF16),
            jax.ShapeDtypeStruct((m, LANES), F32),
        ],
        compiler_params=_params(("parallel",)),
        name="norm_gates",
    )(xa, xb, gain, w_hi, w_lo, bias)


_CAST_ROWS = 256


def _mm_cast_body(shift, x_ref, w_ref, wn_ref, o_ref, w_s):
    @pl.when(pl.program_id(1) == 0)
    def _():
        k, tn = w_s.shape
        for r0 in range(0, k, _CAST_ROWS):
            rows = slice(r0, r0 + _CAST_ROWS)
            if shift:
                both = jnp.concatenate([w_ref[rows, :], wn_ref[rows, :]], axis=1)
                w_s[rows, :] = both[:, shift:shift + tn].astype(BF16)
            else:
                w_s[rows, :] = w_ref[rows, :].astype(BF16)
    o_ref[...] = _dot(x_ref[...], w_s[...]).astype(o_ref.dtype)


def _in_proj(h, w, tm, tn, col0, ncols):
    m, k = h.shape
    shift = col0 % LANES
    base = col0 - shift
    assert base % tn == 0 and ncols % tn == 0 and k % _CAST_ROWS == 0
    return pl.pallas_call(
        functools.partial(_mm_cast_body, shift),
        grid=(ncols // tn, m // tm),
        in_specs=[
            pl.BlockSpec((tm, k), lambda j, i: (i, 0)),
            pl.BlockSpec((k, tn), lambda j, i: (0, base // tn + j)),
            pl.BlockSpec((k, LANES), lambda j, i: (0, (base + (j + 1) * tn) // LANES)),
        ],
        out_specs=pl.BlockSpec((tm, tn), lambda j, i: (i, j)),
        out_shape=jax.ShapeDtypeStruct((m, ncols), F32),
        scratch_shapes=[pltpu.VMEM((k, tn), BF16)],
        compiler_params=_params(("parallel", "arbitrary")),
        name="in_proj",
    )(h, w, w)


def _causal_conv_silu(buf, x_ref, w_ref, b_ref, c):
    buf[SUBLANES:SUBLANES + c, :] = x_ref[...]
    acc = b_ref[...] + w_ref[0:1, :] * buf[5:5 + c, :]
    acc = acc + w_ref[1:2, :] * buf[6:6 + c, :]
    acc = acc + w_ref[2:3, :] * buf[7:7 + c, :]
    acc = acc + w_ref[3:4, :] * buf[8:8 + c, :]
    buf[0:SUBLANES, :] = buf[c:c + SUBLANES, :]
    return acc * jax.nn.sigmoid(acc)


def _mlstm_body(c, dk, nchunks, qpre_ref, kpre_ref, v_ref, oa_ref, gcol_ref, grow_ref,
                cq0_ref, ck0_ref, cwq_ref, cwk_ref, cbq_ref, cbk_ref, mn_ref,
                c0_ref, n0_ref, m0_ref, *rest):
    ha_ref, cout_ref, nout_ref, mout_ref, c_s, n_s, m_s, qbuf, kbuf = rest[-9:]
    h = pl.program_id(1)
    j = pl.program_id(2)

    @pl.when(j == 0)
    def _():
        c_s[...] = c0_ref[0, 0]
        n_s[...] = n0_ref[0, 0]
        m_s[...] = m0_ref[0, 0]
        qbuf[0:SUBLANES, :] = cq0_ref[0]
        kbuf[0:SUBLANES, :] = ck0_ref[0]

    q = _causal_conv_silu(qbuf, qpre_ref, cwq_ref, cbq_ref, c)
    k = _causal_conv_silu(kbuf, kpre_ref, cwk_ref, cbk_ref, c) * (dk ** -0.5)

    gc = gcol_ref[...]
    lane = lax.broadcasted_iota(jnp.int32, gc.shape, 1)
    ig_col = jnp.sum(jnp.where(lane == h, gc, 0.0), axis=1, keepdims=True)
    lf_col = jnp.sum(jnp.where(lane == h + H_A, gc, 0.0), axis=1, keepdims=True)
    gr = grow_ref[0]
    sub = lax.broadcasted_iota(jnp.int32, gr.shape, 0)
    ig_row = jnp.sum(jnp.where(sub == h, gr, 0.0), axis=0, keepdims=True)
    lf_row = jnp.sum(jnp.where(sub == h + H_A, gr, 0.0), axis=0, keepdims=True)

    t_i = lax.broadcasted_iota(jnp.int32, (c, c), 0)
    s_i = lax.broadcasted_iota(jnp.int32, (c, c), 1)
    causal = s_i <= t_i
    b_col = jnp.sum(jnp.where(causal, lf_row, 0.0), axis=1, keepdims=True)
    b_row = jnp.sum(jnp.where(t_i <= s_i, lf_col, 0.0), axis=0, keepdims=True)
    m_prev = m_s[0:1, 0:1]
    log_d = jnp.where(causal, b_col - b_row + ig_row, -jnp.inf)
    inter = b_col + m_prev
    m_tok = jnp.maximum(inter, jnp.max(log_d, axis=1, keepdims=True))
    dw = jnp.exp(log_d - m_tok)
    a_inter = jnp.exp(inter - m_tok)

    qb = q.astype(BF16)
    kb = k.astype(BF16)
    vb = v_ref[...].astype(BF16)
    s = _dot_nt(qb, kb) * dw
    c_old = c_s[...]
    num = a_inter * _dot(qb, c_old.astype(BF16)) + _dot(s.astype(BF16), vb)
    den = a_inter * jnp.sum(q * n_s[...], axis=1, keepdims=True) + jnp.sum(s, axis=1, keepdims=True)
    den = jnp.maximum(jnp.abs(den), jnp.exp(-m_tok))
    hh = num * (1.0 / den)

    b_last = jnp.sum(lf_row, axis=1, keepdims=True)
    logw = b_last - b_col + ig_col
    m_new = jnp.maximum(b_last + m_prev, jnp.max(logw, axis=0, keepdims=True))
    kw = k * jnp.exp(logw - m_new)
    decay = jnp.exp(b_last + m_prev - m_new)
    c_s[...] = decay * c_old + _dot_tn(kw.astype(BF16), vb)
    n_s[...] = decay * n_s[...] + jnp.sum(kw, axis=0, keepdims=True)
    m_s[...] = jnp.broadcast_to(m_new, m_s.shape)

    hn = hh * lax.rsqrt(jnp.mean(hh * hh, axis=-1, keepdims=True) + EPS) * mn_ref[...]
    ha_ref[...] = (hn * jax.nn.sigmoid(oa_ref[...])).astype(BF16)

    @pl.when(j == nchunks - 1)
    def _():
        cout_ref[0, 0] = c_s[...]
        nout_ref[0, 0] = n_s[...]
        mout_ref[0, 0] = m_s[...]


_DK = 256
_DV = 512
_Z1_QK, _Z1_VA = 0, 2048
_Z2_OA, _Z2_QB, _Z2_KB, _Z2_VB, _Z2_GB, _Z2_GA, _Z2_GBM = 0, 2048, 4096, 6144, 10240, 14336, 16384


def _mlstm_scan(z1, z2, gates, grow, conv0, conv_w8, conv_b, mnorm, c0, n0, m0, *, row0, nb, t, c,
                share_state, ha_prev, m_total):
    nchunks = t // c
    zr = row0 // c
    sb = (lambda b: 0) if share_state else (lambda b: b)
    in_specs = [
        pl.BlockSpec((c, _DK), lambda b, h, j: (zr + b * nchunks + j, _Z1_QK // _DK + h)),
        pl.BlockSpec((c, _DK), lambda b, h, j: (zr + b * nchunks + j, _Z1_QK // _DK + H_A + h)),
        pl.BlockSpec((c, _DV), lambda b, h, j: (zr + b * nchunks + j, _Z1_VA // _DV + h)),
        pl.BlockSpec((c, _DV), lambda b, h, j: (zr + b * nchunks + j, _Z2_OA // _DV + h)),
        pl.BlockSpec((c, LANES), lambda b, h, j: (zr + b * nchunks + j, 0)),
        pl.BlockSpec((1, SUBLANES, c), lambda b, h, j: (b, 0, j)),
        pl.BlockSpec((1, SUBLANES, _DK), lambda b, h, j: (sb(b), 0, h)),
        pl.BlockSpec((1, SUBLANES, _DK), lambda b, h, j: (sb(b), 0, H_A + h)),
        pl.BlockSpec((SUBLANES, _DK), lambda b, h, j: (0, h)),
        pl.BlockSpec((SUBLANES, _DK), lambda b, h, j: (0, H_A + h)),
        pl.BlockSpec((1, _DK), lambda b, h, j: (0, h)),
        pl.BlockSpec((1, _DK), lambda b, h, j: (0, H_A + h)),
        pl.BlockSpec((1, _DV), lambda b, h, j: (0, h)),
        pl.BlockSpec((1, 1, _DK, _DV), lambda b, h, j: (sb(b), h, 0, 0)),
        pl.BlockSpec((1, 1, 1, _DK), lambda b, h, j: (sb(b), h, 0, 0)),
        pl.BlockSpec((1, 1, SUBLANES, LANES), lambda b, h, j: (sb(b), h, 0, 0)),
    ]
    args = [z1, z1, z1, z2, gates, grow, conv0, conv0, conv_w8, conv_w8, conv_b, conv_b, mnorm, c0, n0, m0]
    aliases = {}
    if ha_prev is not None:
        in_specs.append(pl.BlockSpec(memory_space=pl.ANY))
        args.append(ha_prev)
        aliases = {len(args) - 1: 0}
    out_specs = [
        pl.BlockSpec((c, _DV), lambda b, h, j: (zr + b * nchunks + j, h)),
        pl.BlockSpec((1, 1, _DK, _DV), lambda b, h, j: (b, h, 0, 0)),
        pl.BlockSpec((1, 1, 1, _DK), lambda b, h, j: (b, h, 0, 0)),
        pl.BlockSpec((1, 1, SUBLANES, LANES), lambda b, h, j: (b, h, 0, 0)),
    ]
    out_shape = [
        jax.ShapeDtypeStruct((m_total, H_A * _DV), BF16),
        jax.ShapeDtypeStruct((nb, H_A, _DK, _DV), F32),
        jax.ShapeDtypeStruct((nb, H_A, 1, _DK), F32),
        jax.ShapeDtypeStruct((nb, H_A, SUBLANES, LANES), F32),
    ]
    return pl.pallas_call(
        functools.partial(_mlstm_body, c, _DK, nchunks),
        grid=(nb, H_A, nchunks),
        in_specs=in_specs,
        out_specs=out_specs,
        out_shape=out_shape,
        scratch_shapes=[
            pltpu.VMEM((_DK, _DV), F32),
            pltpu.VMEM((1, _DK), F32),
            pltpu.VMEM((SUBLANES, LANES), F32),
            pltpu.VMEM((c + SUBLANES, _DK), F32),
            pltpu.VMEM((c + SUBLANES, _DK), F32),
        ],
        input_output_aliases=aliases,
        compiler_params=_params(("parallel", "parallel", "arbitrary")),
        name="mlstm_scan",
    )(*args)


def _rotary(x, cos, sin):
    half = x.shape[-1] // 2
    x1 = x[:, :half]
    x2 = x[:, half:]
    return jnp.concatenate([x1 * cos - x2 * sin, x2 * cos + x1 * sin], axis=-1)


def _ret_body(c, dk, nchunks, lg_ref, q_ref, k_ref, v_ref, g_ref, cos_ref, sin_ref, rn_ref, s0_ref, *rest):
    ob_ref, sout_ref, s_s = rest[-3:]
    h = pl.program_id(1)
    j = pl.program_id(2)

    @pl.when(j == 0)
    def _():
        s_s[...] = s0_ref[0, 0]

    lg = lg_ref[h]
    cos = cos_ref[...]
    sin = sin_ref[...]
    qb = _rotary(q_ref[...], cos, sin).astype(BF16)
    kr = _rotary(k_ref[...], cos, sin) * (dk ** -0.5)
    vb = v_ref[...].astype(BF16)

    t_i = lax.broadcasted_iota(jnp.int32, (c, c), 0)
    s_i = lax.broadcasted_iota(jnp.int32, (c, c), 1)
    causal = t_i >= s_i
    diff = jnp.where(causal, t_i - s_i, 0).astype(F32)
    dmat = jnp.where(causal, jnp.exp(lg * diff), 0.0)
    t_col = lax.broadcasted_iota(jnp.int32, (c, 1), 0).astype(F32)

    scores = _dot_nt(qb, kr.astype(BF16)) * dmat
    s_old = s_s[...]
    cross = jnp.exp(lg * (t_col + 1.0))
    o = _dot(scores.astype(BF16), vb) + cross * _dot(qb, s_old.astype(BF16))
    wk = jnp.exp(lg * (c - 1.0 - t_col))
    s_s[...] = jnp.exp(lg * c) * s_old + _dot_tn((kr * wk).astype(BF16), vb)

    mu = jnp.mean(o, axis=-1, keepdims=True)
    oc = o - mu
    var = jnp.mean(oc * oc, axis=-1, keepdims=True)
    g = g_ref[...]
    ob_ref[...] = (oc * lax.rsqrt(var + EPS) * rn_ref[...] * (g * jax.nn.sigmoid(g))).astype(BF16)

    @pl.when(j == nchunks - 1)
    def _():
        sout_ref[0, 0] = s_s[...]


def _ret_scan(log_g, z2, cos, sin, rnorm, s0, *, row0, nb, t, c, share_state, ob_prev, m_total):
    nchunks = t // c
    zr = row0 // c
    sb = (lambda b: 0) if share_state else (lambda b: b)
    in_specs = [
        pl.BlockSpec((c, _DK), lambda b, h, j, lg: (zr + b * nchunks + j, _Z2_QB // _DK + h)),
        pl.BlockSpec((c, _DK), lambda b, h, j, lg: (zr + b * nchunks + j, _Z2_KB // _DK + h)),
        pl.BlockSpec((c, _DV), lambda b, h, j, lg: (zr + b * nchunks + j, _Z2_VB // _DV + h)),
        pl.BlockSpec((c, _DV), lambda b, h, j, lg: (zr + b * nchunks + j, _Z2_GB // _DV + h)),
        pl.BlockSpec((c, _DK // 2), lambda b, h, j, lg: (j, 0)),
        pl.BlockSpec((c, _DK // 2), lambda b, h, j, lg: (j, 0)),
        pl.BlockSpec((1, _DV), lambda b, h, j, lg: (0, h)),
        pl.BlockSpec((1, 1, _DK, _DV), lambda b, h, j, lg: (sb(b), h, 0, 0)),
    ]
    args = [z2, z2, z2, z2, cos, sin, rnorm, s0]
    aliases = {}
    if ob_prev is not None:
        in_specs.append(pl.BlockSpec(memory_space=pl.ANY))
        args.append(ob_prev)
        aliases = {len(args): 0}
    grid_spec = pltpu.PrefetchScalarGridSpec(
        num_scalar_prefetch=1,
        grid=(nb, H_B, nchunks),
        in_specs=in_specs,
        out_specs=[
            pl.BlockSpec((c, _DV), lambda b, h, j, lg: (zr + b * nchunks + j, h)),
            pl.BlockSpec((1, 1, _DK, _DV), lambda b, h, j, lg: (b, h, 0, 0)),
        ],
        scratch_shapes=[pltpu.VMEM((_DK, _DV), F32)],
    )
    return pl.pallas_call(
        functools.partial(_ret_body, c, _DK, nchunks),
        grid_spec=grid_spec,
        out_shape=[
            jax.ShapeDtypeStruct((m_total, H_B * _DV), BF16),
            jax.ShapeDtypeStruct((nb, H_B, _DK, _DV), F32),
        ],
        input_output_aliases=aliases,
        compiler_params=_params(("parallel", "parallel", "arbitrary")),
        name="ret_scan",
    )(log_g, *args)


def _merge_body(ha_ref, ob_ref, wa_ref, wb_ref, ga_ref, gb_ref, o_ref):
    ya = _dot(ha_ref[...], wa_ref[...])
    yb = _dot(ob_ref[...], wb_ref[...])
    o_ref[...] = (jax.nn.sigmoid(ga_ref[...]) * ya + jax.nn.sigmoid(gb_ref[...]) * yb).astype(o_ref.dtype)


def _merge(ha, ob, wa, wb, z2, tm, tn):
    m = ha.shape[0]
    d = wa.shape[1]
    return pl.pallas_call(
        _merge_body,
        grid=(d // tn, m // tm),
        in_specs=[
            pl.BlockSpec((tm, ha.shape[1]), lambda j, i: (i, 0)),
            pl.BlockSpec((tm, ob.shape[1]), lambda j, i: (i, 0)),
            pl.BlockSpec((wa.shape[0], tn), lambda j, i: (0, j)),
            pl.BlockSpec((wb.shape[0], tn), lambda j, i: (0, j)),
            pl.BlockSpec((tm, tn), lambda j, i: (i, _Z2_GA // tn + j)),
            pl.BlockSpec((tm, tn), lambda j, i: (i, _Z2_GBM // tn + j)),
        ],
        out_specs=pl.BlockSpec((tm, tn), lambda j, i: (i, j)),
        out_shape=jax.ShapeDtypeStruct((m, d), BF16),
        compiler_params=_params(("parallel", "parallel")),
        name="merge",
    )(ha, ob, wa, wb, z2, z2)


def _wo_router_body(n_first, mg_ref, wo_ref, xa_ref, xb_ref, g2_ref, wrh_ref, wrl_ref, br_ref,
                    x1_ref, h2_ref, idx_ref, wt_ref):
    def run(x_ref):
        x1 = x_ref[...] + _dot(mg_ref[...], wo_ref[...])
        x1_ref[...] = x1
        h2 = x1 * lax.rsqrt(jnp.mean(x1 * x1, axis=-1, keepdims=True) + EPS) * g2_ref[...]
        h2_ref[...] = h2
        logits = _dot3(h2, wrh_ref[...], wrl_ref[...]) + br_ref[...]
        lane = lax.broadcasted_iota(jnp.int32, logits.shape, 1)
        cur = jnp.where(lane < N_EXPERTS, logits, -jnp.inf)
        vals, idxs = [], []
        for _ in range(TOP_K):
            mx = jnp.max(cur, axis=1, keepdims=True)
            ix = jnp.min(jnp.where(cur == mx, lane, LANES), axis=1, keepdims=True)
            vals.append(mx)
            idxs.append(ix)
            cur = jnp.where(lane == ix, -jnp.inf, cur)
        exps = [jnp.exp(v - vals[0]) for v in vals]
        tot = exps[0] + exps[1] + exps[2] + exps[3]
        idx_out = jnp.zeros(logits.shape, jnp.int32)
        wt_out = jnp.zeros(logits.shape, F32)
        for kk in range(TOP_K):
            idx_out = jnp.where(lane == kk, idxs[kk], idx_out)
            wt_out = jnp.where(lane == kk, exps[kk] / tot, wt_out)
        idx_ref[...] = idx_out
        wt_ref[...] = wt_out

    i = pl.program_id(0)
    pl.when(i < n_first)(lambda: run(xa_ref))
    pl.when(i >= n_first)(lambda: run(xb_ref))


def _wo_router(merged, wo, xa, xb, g2, wr_hi, wr_lo, br, tm):
    m, d = merged.shape
    n_first = xa.shape[0] // tm
    return pl.pallas_call(
        functools.partial(_wo_router_body, n_first),
        grid=(m // tm,),
        in_specs=[
            pl.BlockSpec((tm, d), lambda i: (i, 0)),
            pl.BlockSpec((d, d), lambda i: (0, 0)),
        ] + _two_group_specs(tm, d, n_first) + [
            pl.BlockSpec((1, d), lambda i: (0, 0)),
            pl.BlockSpec((d, LANES), lambda i: (0, 0)),
            pl.BlockSpec((d, LANES), lambda i: (0, 0)),
            pl.BlockSpec((1, LANES), lambda i: (0, 0)),
        ],
        out_specs=[
            pl.BlockSpec((tm, d), lambda i: (i, 0)),
            pl.BlockSpec((tm, d), lambda i: (i, 0)),
            pl.BlockSpec((tm, LANES), lambda i: (i, 0)),
            pl.BlockSpec((tm, LANES), lambda i: (i, 0)),
        ],
        out_shape=[
            jax.ShapeDtypeStruct((m, d), F32),
            jax.ShapeDtypeStruct((m, d), F32),
            jax.ShapeDtypeStruct((m, LANES), jnp.int32),
            jax.ShapeDtypeStruct((m, LANES), F32),
        ],
        compiler_params=_params(("parallel",)),
        name="wo_router",
    )(merged, wo, xa, xb, g2, wr_hi, wr_lo, br)


def _dispatch_body(dest_ref, h_ref, xs_hbm, sem):
    tm = h_ref.shape[0]

    def row_copy(t, kk):
        return pltpu.make_async_copy(h_ref.at[pl.ds(t, 1)],
                                     xs_hbm.at[pl.ds(dest_ref[0, 0, t * TOP_K + kk], 1)], sem)

    def start(t, carry):
        for kk in range(TOP_K):
            row_copy(t, kk).start()
        return carry
    lax.fori_loop(0, tm, start, 0)

    def wait(t, carry):
        for kk in range(TOP_K):
            row_copy(t, kk).wait()
        return carry
    lax.fori_loop(0, tm, wait, 0)


def _dispatch(dest3, h2, tm):
    n, d = h2.shape
    return pl.pallas_call(
        _dispatch_body,
        grid=(n // tm,),
        in_specs=[
            pl.BlockSpec((1, 1, tm * TOP_K), lambda i: (i, 0, 0), memory_space=pltpu.SMEM),
            pl.BlockSpec((tm, d), lambda i: (i, 0)),
        ],
        out_specs=pl.BlockSpec(memory_space=pl.ANY),
        out_shape=jax.ShapeDtypeStruct((n * TOP_K, d), h2.dtype),
        scratch_shapes=[pltpu.SemaphoreType.DMA(())],
        compiler_params=_params(("arbitrary",)),
        name="moe_dispatch",
    )(dest3, h2)


def _item_flags(t, ie_ref, fb_ref, ni_ref):
    active = t < ni_ref[0]
    new_expert = jnp.logical_or(t == 0, ie_ref[t] != ie_ref[jnp.maximum(t - 1, 0)])
    return active, jnp.logical_and(active, new_expert), fb_ref[t] == 1


def _row_mask(rows, t, lo_ref, hi_ref):
    row = lax.broadcasted_iota(jnp.int32, (rows, 1), 0)
    return jnp.logical_and(row >= lo_ref[t], row < hi_ref[t])


def _stream_expert_weights(t, f, new_expert, ie_ref, grp_ref, gexp_ref, ng_ref, hbm_refs, land_refs, bf_refs, sem):
    tn = bf_refs[0].shape[1]

    def copies(e, ff):
        col = pl.multiple_of(ff * tn, tn)
        return [pltpu.make_async_copy(hbm.at[e, :, pl.ds(col, tn)], land, sem.at[k])
                for k, (hbm, land) in enumerate(zip(hbm_refs, land_refs))]

    @pl.when(jnp.logical_and(t == 0, f == 0))
    def _():
        for cp in copies(ie_ref[0], 0):
            cp.start()

    @pl.when(new_expert)
    def _():
        for cp in copies(ie_ref[t], f):
            cp.wait()
        for land, bf in zip(land_refs, bf_refs):
            bf[...] = land[...].astype(BF16)
        g = grp_ref[t]
        last = g + 1 >= ng_ref[0]
        nxt_e = gexp_ref[jnp.where(last, 0, g + 1)]
        nxt_f = jnp.where(last, f + 1, f)

        @pl.when(jnp.logical_or(jnp.logical_not(last), f + 1 < pl.num_programs(0)))
        def _():
            for cp in copies(nxt_e, nxt_f):
                cp.start()


def _expert_up_body(ib_ref, ie_ref, lo_ref, hi_ref, fb_ref, ni_ref, grp_ref, gexp_ref, ng_ref,
                    x_ref, wg_hbm, wu_hbm, bg_ref, bu_ref, a_ref, wg_l, wu_l, wg_s, wu_s, sem):
    f = pl.program_id(0)
    t = pl.program_id(1)
    active, new_expert, first_visit = _item_flags(t, ie_ref, fb_ref, ni_ref)
    _stream_expert_weights(t, f, new_expert, ie_ref, grp_ref, gexp_ref, ng_ref,
                           (wg_hbm, wu_hbm), (wg_l, wu_l), (wg_s, wu_s), sem)

    def compute():
        x = x_ref[...].astype(BF16)
        g = jnp.minimum(_dot(x, wg_s[...]) + bg_ref[0], SWIGLU_LIMIT)
        u = jnp.clip(_dot(x, wu_s[...]) + bu_ref[0], -SWIGLU_LIMIT, SWIGLU_LIMIT)
        return (g * jax.nn.sigmoid(SWIGLU_ALPHA * g) * (u + 1.0)).astype(BF16)

    mask = _row_mask(x_ref.shape[0], t, lo_ref, hi_ref)

    @pl.when(jnp.logical_and(active, first_visit))
    def _():
        a_ref[...] = jnp.where(mask, compute(), jnp.zeros(a_ref.shape, BF16))

    @pl.when(jnp.logical_and(active, jnp.logical_not(first_visit)))
    def _():
        a_ref[...] = jnp.where(mask, compute(), a_ref[...])


def _item_spec(shape, fn):
    def index_map(f, t, ib, ie, lo, hi, fb, ni, *_):
        tc = jnp.minimum(t, ni[0] - 1)
        return fn(tc, f, ib, ie)
    return pl.BlockSpec(shape, index_map)


_N_TABLES = 9


def _expert_up(tables, xs, w_gate, w_up, b_gate, b_up, rows, tf):
    p, d = xs.shape
    dff = w_gate.shape[2]
    n_items = tables[0].shape[0]
    grid_spec = pltpu.PrefetchScalarGridSpec(
        num_scalar_prefetch=_N_TABLES,
        grid=(dff // tf, n_items),
        in_specs=[
            _item_spec((rows, d), lambda t, f, ib, ie: (ib[t], 0)),
            pl.BlockSpec(memory_space=pl.ANY),
            pl.BlockSpec(memory_space=pl.ANY),
            _item_spec((1, 1, tf), lambda t, f, ib, ie: (ie[t], 0, f)),
            _item_spec((1, 1, tf), lambda t, f, ib, ie: (ie[t], 0, f)),
        ],
        out_specs=_item_spec((rows, tf), lambda t, f, ib, ie: (ib[t], f)),
        scratch_shapes=[pltpu.VMEM((d, tf), F32), pltpu.VMEM((d, tf), F32),
                        pltpu.VMEM((d, tf), BF16), pltpu.VMEM((d, tf), BF16),
                        pltpu.SemaphoreType.DMA((2,))],
    )
    return pl.pallas_call(
        _expert_up_body,
        grid_spec=grid_spec,
        out_shape=jax.ShapeDtypeStruct((p, dff), BF16),
        compiler_params=_params(("arbitrary", "arbitrary")),
        name="expert_up",
    )(*tables, xs, w_gate, w_up, b_gate, b_up)


def _expert_down_body(ib_ref, ie_ref, lo_ref, hi_ref, fb_ref, ni_ref, grp_ref, gexp_ref, ng_ref,
                      a_ref, wd_hbm, bd_ref, o_ref, wd_l, wd_s, sem):
    f = pl.program_id(0)
    t = pl.program_id(1)
    active, new_expert, first_visit = _item_flags(t, ie_ref, fb_ref, ni_ref)
    _stream_expert_weights(t, f, new_expert, ie_ref, grp_ref, gexp_ref, ng_ref,
                           (wd_hbm,), (wd_l,), (wd_s,), sem)

    def compute():
        return _dot(a_ref[...], wd_s[...]) + bd_ref[0]

    mask = _row_mask(a_ref.shape[0], t, lo_ref, hi_ref)

    @pl.when(jnp.logical_and(active, first_visit))
    def _():
        o_ref[...] = jnp.where(mask, compute(), 0.0)

    @pl.when(jnp.logical_and(active, jnp.logical_not(first_visit)))
    def _():
        o_ref[...] = jnp.where(mask, compute(), o_ref[...])


def _expert_down(tables, a, w_down, b_down, rows, tn):
    p, dff = a.shape
    d = w_down.shape[2]
    n_items = tables[0].shape[0]
    grid_spec = pltpu.PrefetchScalarGridSpec(
        num_scalar_prefetch=_N_TABLES,
        grid=(d // tn, n_items),
        in_specs=[
            _item_spec((rows, dff), lambda t, f, ib, ie: (ib[t], 0)),
            pl.BlockSpec(memory_space=pl.ANY),
            _item_spec((1, 1, tn), lambda t, f, ib, ie: (ie[t], 0, f)),
        ],
        out_specs=_item_spec((rows, tn), lambda t, f, ib, ie: (ib[t], f)),
        scratch_shapes=[pltpu.VMEM((dff, tn), F32), pltpu.VMEM((dff, tn), BF16),
                        pltpu.SemaphoreType.DMA((1,))],
    )
    return pl.pallas_call(
        _expert_down_body,
        grid_spec=grid_spec,
        out_shape=jax.ShapeDtypeStruct((p, d), F32),
        compiler_params=_params(("arbitrary", "arbitrary")),
        name="expert_down",
    )(*tables, a, w_down, b_down)


def _combine_body(dest_ref, wt_ref, x1_ref, fn_ref, rows_hbm, y_ref, buf, sem):
    tm = x1_ref.shape[0]

    def row_copy(t, kk):
        return pltpu.make_async_copy(rows_hbm.at[pl.ds(dest_ref[0, 0, t * TOP_K + kk], 1)],
                                     buf.at[kk, pl.ds(t, 1)], sem)

    def start(t, carry):
        for kk in range(TOP_K):
            row_copy(t, kk).start()
        return carry
    lax.fori_loop(0, tm, start, 0)

    def wait(t, carry):
        for kk in range(TOP_K):
            row_copy(t, kk).wait()
        return carry
    lax.fori_loop(0, tm, wait, 0)

    wt = wt_ref[...]
    moe = wt[:, 0:1] * buf[0]
    for kk in range(1, TOP_K):
        moe = moe + wt[:, kk:kk + 1] * buf[kk]
    y = x1_ref[...] + moe
    y_ref[...] = y * lax.rsqrt(jnp.mean(y * y, axis=-1, keepdims=True) + EPS) * fn_ref[...]


def _combine(dest3, wts, x1, final_norm, out_rows, *, tile0, ntiles, tm):
    d = x1.shape[1]
    return pl.pallas_call(
        _combine_body,
        grid=(ntiles,),
        in_specs=[
            pl.BlockSpec((1, 1, tm * TOP_K), lambda i: (tile0 + i, 0, 0), memory_space=pltpu.SMEM),
            pl.BlockSpec((tm, LANES), lambda i: (tile0 + i, 0)),
            pl.BlockSpec((tm, d), lambda i: (tile0 + i, 0)),
            pl.BlockSpec((1, d), lambda i: (0, 0)),
            pl.BlockSpec(memory_space=pl.ANY),
        ],
        out_specs=pl.BlockSpec((tm, d), lambda i: (i, 0)),
        out_shape=jax.ShapeDtypeStruct((ntiles * tm, d), F32),
        scratch_shapes=[pltpu.VMEM((TOP_K, tm, d), F32), pltpu.SemaphoreType.DMA(())],
        compiler_params=_params(("arbitrary",)),
        name="moe_combine",
    )(dest3, wts, x1, final_norm, out_rows)


def _routing_tables(top_idx, rows):
    n = top_idx.shape[0]
    m = n * TOP_K
    experts = jnp.arange(N_EXPERTS, dtype=jnp.int32)
    e_flat = top_idx.reshape(m)
    onehot = (e_flat[:, None] == experts[None, :]).astype(jnp.int32)
    csum = jnp.cumsum(onehot, axis=0)
    counts = csum[-1]
    ends = jnp.cumsum(counts)
    starts = ends - counts
    dest = jnp.sum(onehot * (csum - 1 + starts[None, :]), axis=1)

    first_blk = starts // rows
    last_blk = (ends - 1) // rows
    n_it = jnp.where(counts > 0, last_blk - first_blk + 1, 0)
    it_end = jnp.cumsum(n_it)
    it_start = it_end - n_it
    n_items = it_end[-1:]
    max_items = m // rows + N_EXPERTS
    t = jnp.minimum(jnp.arange(max_items, dtype=jnp.int32), n_items[0] - 1)
    ie = jnp.sum((it_end[None, :] <= t[:, None]).astype(jnp.int32), axis=1)
    sel = (ie[:, None] == experts[None, :]).astype(jnp.int32)
    pick = lambda v: jnp.sum(sel * v[None, :], axis=1)
    ib = pick(first_blk) + t - pick(it_start)
    lo = jnp.clip(pick(starts) - ib * rows, 0, rows)
    hi = jnp.clip(pick(ends) - ib * rows, 0, rows)
    fb = jnp.concatenate([jnp.ones((1,), jnp.int32), (ib[1:] != ib[:-1]).astype(jnp.int32)])
    has = (counts > 0).astype(jnp.int32)
    pos = jnp.cumsum(has) - 1
    grp = pick(pos)
    gexp = jnp.sum(((pos[None, :] == experts[:, None]) & (has[None, :] == 1)).astype(jnp.int32) * experts[None, :],
                   axis=1)
    n_groups = jnp.sum(has, keepdims=True)
    i32 = lambda v: v.astype(jnp.int32)
    return i32(dest), (i32(ib), i32(ie), i32(lo), i32(hi), fb, i32(n_items), i32(grp), i32(gexp), i32(n_groups))


def _pad_rows8(a, at):
    b, r, n = a.shape
    return jnp.zeros((b, SUBLANES, n), a.dtype).at[:, at:at + r].set(a)


def _rope_tables(pos, half):
    inv = jnp.power(ROPE_BASE, -jnp.arange(half, dtype=F32) / half)
    ang = pos.astype(F32)[:, None] * inv[None, :]
    return jnp.cos(ang), jnp.sin(ang)


def kernel(x_prompt, x_sample, state_mlstm_C, state_mlstm_n, state_mlstm_m, state_mlstm_conv, state_ret,
           meta_tokens, norm1, w_in, b_igate, b_fgate, conv_w, conv_b, mlstm_norm, ret_norm,
           w_out_a, w_out_b, w_o, norm2, w_router, b_router, w_gate, b_gate, w_up, b_up,
           w_down, b_down, final_norm):
    bp, tp, d = x_prompt.shape
    bs, ts, _ = x_sample.shape
    assert norm1.shape[0] == 1, "single-layer stack"
    n_p, n_s = bp * tp, bs * ts
    n_tok = n_p + n_s
    qk_cols = 2 * H_A * _DK
    gate_col0 = qk_cols + H_A * _DV

    w_in0 = w_in[0]
    rest_col0 = gate_col0 + 2 * H_A
    rest_cols = w_in0.shape[1] - rest_col0
    w_if =jnp.zeros((d, LANES), F32).at[:, :2 * H_A].set(w_in0[:, gate_col0:gate_col0 + 2 * H_A])
    w_if_hi, w_if_lo = _split_bf16(w_if)
    b_if = jnp.zeros((1, LANES), F32).at[0, :H_A].set(b_igate[0]).at[0, H_A:2 * H_A].set(b_fgate[0])
    w_r = jnp.zeros((d, LANES), F32).at[:, :N_EXPERTS].set(w_router[0])
    w_r_hi, w_r_lo = _split_bf16(w_r)
    b_r = jnp.zeros((1, LANES), F32).at[0, :N_EXPERTS].set(b_router[0])
    conv_w8 = jnp.zeros((SUBLANES, qk_cols), F32).at[:CONV_W].set(conv_w[0])
    conv_b1 = conv_b[0].reshape(1, qk_cols)
    g1 = norm1[0].reshape(1, d)
    g2 = norm2[0].reshape(1, d)
    gf = final_norm.reshape(1, d)
    mnorm = mlstm_norm[0].reshape(1, H_A * _DV)
    rnorm = ret_norm[0].reshape(1, H_B * _DV)
    log_g = jnp.asarray(np.log1p(-np.power(2.0, -5.0 - np.arange(H_B, dtype=np.float32))).astype(np.float32))

    xp2 = x_prompt.reshape(n_p, d)
    xs2 = x_sample.reshape(n_s, d)
    tm = next(t for t in (512, 256, 128) if n_p % t == 0 and n_s % t == 0)
    h_all, gates = _norm_gates(xp2, xs2, g1, w_if_hi, w_if_lo, b_if, tm)
    z1 = _in_proj(h_all, w_in0, tm, 1024, 0, gate_col0)
    z2 = _in_proj(h_all, w_in0, tm, 1024, rest_col0, rest_cols)
    h_meta, gates_meta = _norm_gates(meta_tokens, None, g1, w_if_hi, w_if_lo, b_if, N_META)
    z1_meta = _in_proj(h_meta, w_in0, N_META, 1024, 0, gate_col0)
    z2_meta = _in_proj(h_meta, w_in0, N_META, 1024, rest_col0, rest_cols)

    def gate_rows(g, nb, t):
        return jnp.transpose(g[:, :SUBLANES].reshape(nb, t, SUBLANES), (0, 2, 1))

    zeros_c = jnp.zeros((1, H_A, _DK, _DV), F32)
    zeros_n = jnp.zeros((1, H_A, 1, _DK), F32)
    zeros_m = jnp.zeros((1, H_A, SUBLANES, LANES), F32)
    zeros_conv = jnp.zeros((1, SUBLANES, qk_cols), F32)
    _, c_meta, n_meta, m_meta = _mlstm_scan(
        z1_meta, z2_meta, gates_meta, gate_rows(gates_meta, 1, N_META), zeros_conv, conv_w8, conv_b1, mnorm,
        zeros_c, zeros_n, zeros_m, row0=0, nb=1, t=N_META, c=N_META,
        share_state=False, ha_prev=None, m_total=N_META)
    conv_meta = _pad_rows8(z1_meta[None, N_META - (CONV_W - 1):, :qk_cols], SUBLANES - (CONV_W - 1))
    cp = PROMPT_CHUNK if tp % PROMPT_CHUNK == 0 else tp
    ha, p_c, p_n, p_m = _mlstm_scan(
        z1, z2, gates, gate_rows(gates[:n_p], bp, tp), conv_meta, conv_w8, conv_b1, mnorm,
        c_meta, n_meta, m_meta, row0=0, nb=bp, t=tp, c=cp,
        share_state=True, ha_prev=None, m_total=n_tok)
    conv_s0 = _pad_rows8(state_mlstm_conv[0], SUBLANES - (CONV_W - 1))
    m_s0 = jnp.broadcast_to(state_mlstm_m[0][:, :, None, None], (bs, H_A, SUBLANES, LANES))
    ha, s_c, s_n, s_m = _mlstm_scan(
        z1, z2, gates, gate_rows(gates[n_p:], bs, ts), conv_s0, conv_w8, conv_b1, mnorm,
        state_mlstm_C[0], state_mlstm_n[0].reshape(bs, H_A, 1, _DK), m_s0,
        row0=n_p, nb=bs, t=ts, c=ts, share_state=False, ha_prev=ha, m_total=n_tok)

    cos_m, sin_m = _rope_tables(jnp.arange(N_META, dtype=jnp.int32), _DK // 2)
    cos_p, sin_p = _rope_tables(N_META + jnp.arange(tp, dtype=jnp.int32), _DK // 2)
    cos_s, sin_s = _rope_tables(N_META + PAST_LEN + jnp.arange(ts, dtype=jnp.int32), _DK // 2)
    zeros_s = jnp.zeros((1, H_B, _DK, _DV), F32)
    _, s_meta = _ret_scan(log_g, z2_meta, cos_m, sin_m, rnorm, zeros_s, row0=0, nb=1,
                          t=N_META, c=N_META, share_state=False, ob_prev=None, m_total=N_META)
    ob, p_ret = _ret_scan(log_g, z2, cos_p, sin_p, rnorm, s_meta, row0=0, nb=bp, t=tp, c=cp,
                          share_state=True, ob_prev=None, m_total=n_tok)
    ob, s_ret = _ret_scan(log_g, z2, cos_s, sin_s, rnorm, state_ret[0], row0=n_p, nb=bs,
                          t=ts, c=ts, share_state=False, ob_prev=ob, m_total=n_tok)

    merged = _merge(ha, ob, w_out_a[0].astype(BF16), w_out_b[0].astype(BF16), z2, tm, 512)
    x1, h2, top_idx, wts = _wo_router(merged, w_o[0].astype(BF16), xp2, xs2, g2, w_r_hi, w_r_lo, b_r,
                                      min(tm, 256))

    rows = EXPERT_ROWS
    dest, tables = _routing_tables(top_idx[:, :TOP_K], rows)
    td = DISPATCH_ROWS
    dest3 = dest.reshape(n_tok // td, 1, td * TOP_K)
    xs = _dispatch(dest3, h2, td)
    dff = w_gate.shape[3]
    act = _expert_up(tables, xs, w_gate[0], w_up[0],
                     b_gate[0].reshape(N_EXPERTS, 1, dff), b_up[0].reshape(N_EXPERTS, 1, dff), rows, 1024)
    out_rows = _expert_down(tables, act, w_down[0], b_down[0].reshape(N_EXPERTS, 1, d), rows, 1024)
    y_p = _combine(dest3, wts, x1, gf, out_rows, tile0=0, ntiles=n_p // td, tm=td)
    y_s = _combine(dest3, wts, x1, gf, out_rows, tile0=n_p // td, ntiles=n_s // td, tm=td)

    def tail_rows(row0, nb, t):
        return jnp.stack([z1[row0 + (b + 1) * t - (CONV_W - 1):row0 + (b + 1) * t, :qk_cols] for b in range(nb)])

    p_conv = tail_rows(0, bp, tp)
    s_conv = tail_rows(n_p, bs, ts)
    return (y_p.reshape(bp, tp, d), y_s.reshape(bs, ts, d),
            p_c[None], p_n.reshape(1, bp, H_A, _DK), p_m[None, :, :, 0, 0], p_conv[None], p_ret[None],
            s_c[None], s_n.reshape(1, bs, H_A, _DK), s_m[None, :, :, 0, 0], s_conv[None], s_ret[None])
```

```python
import functools

import numpy as np
import jax
import jax.numpy as jnp
from jax import lax
from jax.experimental import pallas as pl
from jax.experimental.pallas import tpu as pltpu

F32 = jnp.float32
BF16 = jnp.bfloat16

EPS = 1e-5
N_META = 16
PAST_LEN = 1024
H_A = 4
H_B = 8
CONV_W = 4
ROPE_BASE = 10000.0
N_EXPERTS = 32
TOP_K = 4
SWIGLU_LIMIT = 7.0
SWIGLU_ALPHA = 1.702

LANES = 128
SUBLANES = 8
VMEM_LIMIT = 56 * 1024 * 1024
PROMPT_CHUNK = 256
EXPERT_ROWS = 256
DISPATCH_ROWS = 256


def _params(sem, vmem=VMEM_LIMIT):
    return pltpu.CompilerParams(dimension_semantics=sem, vmem_limit_bytes=vmem)


def _dot(a, b):
    return jnp.dot(a, b, preferred_element_type=F32)


def _dot_nt(a, b):
    return lax.dot_general(a, b, (((1,), (1,)), ((), ())), preferred_element_type=F32)


def _dot_tn(a, b):
    return lax.dot_general(a, b, (((0,), (0,)), ((), ())), preferred_element_type=F32)


def _split_bf16(w):
    hi = w.astype(BF16)
    lo = (w - hi.astype(F32)).astype(BF16)
    return hi, lo


def _dot3(a, w_hi, w_lo):
    a_hi = a.astype(BF16)
    a_lo = (a - a_hi.astype(F32)).astype(BF16)
    return _dot(a_hi, w_hi) + _dot(a_hi, w_lo) + _dot(a_lo, w_hi)


def _two_group_specs(tm, d, n_first):
    return [
        pl.BlockSpec((tm, d), lambda i: (jnp.minimum(i, n_first - 1), 0)),
        pl.BlockSpec((tm, d), lambda i: (jnp.maximum(i - n_first, 0), 0)),
    ]


def _norm_gates_body(n_first, xa_ref, xb_ref, g_ref, whi_ref, wlo_ref, b_ref, h_ref, gates_ref):
    def run(x_ref):
        x = x_ref[...]
        y = x * lax.rsqrt(jnp.mean(x * x, axis=-1, keepdims=True) + EPS) * g_ref[...]
        h_ref[...] = y.astype(BF16)
        pre = _dot3(y, whi_ref[...], wlo_ref[...]) + b_ref[...]
        lane = lax.broadcasted_iota(jnp.int32, pre.shape, 1)
        logsig = jnp.minimum(pre, 0.0) - jnp.log1p(jnp.exp(-jnp.abs(pre)))
        gates_ref[...] = jnp.where(lane < H_A, pre, logsig)

    i = pl.program_id(0)
    pl.when(i < n_first)(lambda: run(xa_ref))
    pl.when(i >= n_first)(lambda: run(xb_ref))


def _norm_gates(xa, xb, gain, w_hi, w_lo, bias, tm):
    d = xa.shape[1]
    n_first = xa.shape[0] // tm
    m = xa.shape[0] + (0 if xb is None else xb.shape[0])
    xb = xa if xb is None else xb
    return pl.pallas_call(
        functools.partial(_norm_gates_body, n_first),
        grid=(m // tm,),
        in_specs=_two_group_specs(tm, d, n_first) + [
            pl.BlockSpec((1, d), lambda i: (0, 0)),
            pl.BlockSpec((d, LANES), lambda i: (0, 0)),
            pl.BlockSpec((d, LANES), lambda i: (0, 0)),
            pl.BlockSpec((1, LANES), lambda i: (0, 0)),
        ],
        out_specs=[
            pl.BlockSpec((tm, d), lambda i: (i, 0)),
            pl.BlockSpec((tm, LANES), lambda i: (i, 0)),
        ],
        out_shape=[
            jax.ShapeDtypeStruct((m, d), BF16),
            jax.ShapeDtypeStruct((m, LANES), F32),
        ],
        compiler_params=_params(("parallel",)),
        name="norm_gates",
    )(xa, xb, gain, w_hi, w_lo, bias)


_CAST_ROWS = 256


def _mm_cast_body(shift, x_ref, w_ref, wn_ref, o_ref, w_s):
    @pl.when(pl.program_id(1) == 0)
    def _():
        k, tn = w_s.shape
        for r0 in range(0, k, _CAST_ROWS):
            rows = slice(r0, r0 + _CAST_ROWS)
            if shift:
                both = jnp.concatenate([w_ref[rows, :], wn_ref[rows, :]], axis=1)
                w_s[rows, :] = both[:, shift:shift + tn].astype(BF16)
            else:
                w_s[rows, :] = w_ref[rows, :].astype(BF16)
    o_ref[...] = _dot(x_ref[...], w_s[...]).astype(o_ref.dtype)


def _in_proj(h, w, tm, tn, col0, ncols):
    m, k = h.shape
    shift = col0 % LANES
    base = col0 - shift
    assert base % tn == 0 and ncols % tn == 0 and k % _CAST_ROWS == 0
    return pl.pallas_call(
        functools.partial(_mm_cast_body, shift),
        grid=(ncols // tn, m // tm),
        in_specs=[
            pl.BlockSpec((tm, k), lambda j, i: (i, 0)),
            pl.BlockSpec((k, tn), lambda j, i: (0, base // tn + j)),
            pl.BlockSpec((k, LANES), lambda j, i: (0, (base + (j + 1) * tn) // LANES)),
        ],
        out_specs=pl.BlockSpec((tm, tn), lambda j, i: (i, j)),
        out_shape=jax.ShapeDtypeStruct((m, ncols), F32),
        scratch_shapes=[pltpu.VMEM((k, tn), BF16)],
        compiler_params=_params(("parallel", "arbitrary")),
        name="in_proj",
    )(h, w, w)


def _causal_conv_silu(buf, x_ref, w_ref, b_ref, c):
    buf[SUBLANES:SUBLANES + c, :] = x_ref[...]
    acc = b_ref[...] + w_ref[0:1, :] * buf[5:5 + c, :]
    acc = acc + w_ref[1:2, :] * buf[6:6 + c, :]
    acc = acc + w_ref[2:3, :] * buf[7:7 + c, :]
    acc = acc + w_ref[3:4, :] * buf[8:8 + c, :]
    buf[0:SUBLANES, :] = buf[c:c + SUBLANES, :]
    return acc * jax.nn.sigmoid(acc)


def _mlstm_body(c, dk, nchunks, qpre_ref, kpre_ref, v_ref, oa_ref, gcol_ref, grow_ref,
                cq0_ref, ck0_ref, cwq_ref, cwk_ref, cbq_ref, cbk_ref, mn_ref,
                c0_ref, n0_ref, m0_ref, *rest):
    ha_ref, cout_ref, nout_ref, mout_ref, c_s, n_s, m_s, qbuf, kbuf = rest[-9:]
    h = pl.program_id(1)
    j = pl.program_id(2)

    @pl.when(j == 0)
    def _():
        c_s[...] = c0_ref[0, 0]
        n_s[...] = n0_ref[0, 0]
        m_s[...] = m0_ref[0, 0]
        qbuf[0:SUBLANES, :] = cq0_ref[0]
        kbuf[0:SUBLANES, :] = ck0_ref[0]

    q = _causal_conv_silu(qbuf, qpre_ref, cwq_ref, cbq_ref, c)
    k = _causal_conv_silu(kbuf, kpre_ref, cwk_ref, cbk_ref, c) * (dk ** -0.5)

    gc = gcol_ref[...]
    lane = lax.broadcasted_iota(jnp.int32, gc.shape, 1)
    ig_col = jnp.sum(jnp.where(lane == h, gc, 0.0), axis=1, keepdims=True)
    lf_col = jnp.sum(jnp.where(lane == h + H_A, gc, 0.0), axis=1, keepdims=True)
    gr = grow_ref[0]
    sub = lax.broadcasted_iota(jnp.int32, gr.shape, 0)
    ig_row = jnp.sum(jnp.where(sub == h, gr, 0.0), axis=0, keepdims=True)
    lf_row = jnp.sum(jnp.where(sub == h + H_A, gr, 0.0), axis=0, keepdims=True)

    t_i = lax.broadcasted_iota(jnp.int32, (c, c), 0)
    s_i = lax.broadcasted_iota(jnp.int32, (c, c), 1)
    causal = s_i <= t_i
    b_col = jnp.sum(jnp.where(causal, lf_row, 0.0), axis=1, keepdims=True)
    b_row = jnp.sum(jnp.where(t_i <= s_i, lf_col, 0.0), axis=0, keepdims=True)
    m_prev = m_s[0:1, 0:1]
    log_d = jnp.where(causal, b_col - b_row + ig_row, -jnp.inf)
    inter = b_col + m_prev
    m_tok = jnp.maximum(inter, jnp.max(log_d, axis=1, keepdims=True))
    dw = jnp.exp(log_d - m_tok)
    a_inter = jnp.exp(inter - m_tok)

    qb = q.astype(BF16)
    kb = k.astype(BF16)
    vb = v_ref[...].astype(BF16)
    s = _dot_nt(qb, kb) * dw
    c_old = c_s[...]
    num = a_inter * _dot(qb, c_old.astype(BF16)) + _dot(s.astype(BF16), vb)
    den = a_inter * jnp.sum(q * n_s[...], axis=1, keepdims=True) + jnp.sum(s, axis=1, keepdims=True)
    den = jnp.maximum(jnp.abs(den), jnp.exp(-m_tok))
    hh = num * (1.0 / den)

    b_last = jnp.sum(lf_row, axis=1, keepdims=True)
    logw = b_last - b_col + ig_col
    m_new = jnp.maximum(b_last + m_prev, jnp.max(logw, axis=0, keepdims=True))
    kw = k * jnp.exp(logw - m_new)
    decay = jnp.exp(b_last + m_prev - m_new)
    c_s[...] = decay * c_old + _dot_tn(kw.astype(BF16), vb)
    n_s[...] = decay * n_s[...] + jnp.sum(kw, axis=0, keepdims=True)
    m_s[...] = jnp.broadcast_to(m_new, m_s.shape)

    hn = hh * lax.rsqrt(jnp.mean(hh * hh, axis=-1, keepdims=True) + EPS) * mn_ref[...]
    ha_ref[...] = (hn * jax.nn.sigmoid(oa_ref[...])).astype(BF16)

    @pl.when(j == nchunks - 1)
    def _():
        cout_ref[0, 0] = c_s[...]
        nout_ref[0, 0] = n_s[...]
        mout_ref[0, 0] = m_s[...]


_DK = 256
_DV = 512
_Z1_QK, _Z1_VA = 0, 2048
_Z2_OA, _Z2_QB, _Z2_KB, _Z2_VB, _Z2_GB, _Z2_GA, _Z2_GBM = 0, 2048, 4096, 6144, 10240, 14336, 16384


def _mlstm_scan(z1, z2, gates, grow, conv0, conv_w8, conv_b, mnorm, c0, n0, m0, *, row0, nb, t, c,
                share_state, ha_prev, m_total):
    nchunks = t // c
    zr = row0 // c
    sb = (lambda b: 0) if share_state else (lambda b: b)
    in_specs = [
        pl.BlockSpec((c, _DK), lambda b, h, j: (zr + b * nchunks + j, _Z1_QK // _DK + h)),
        pl.BlockSpec((c, _DK), lambda b, h, j: (zr + b * nchunks + j, _Z1_QK // _DK + H_A + h)),
        pl.BlockSpec((c, _DV), lambda b, h, j: (zr + b * nchunks + j, _Z1_VA // _DV + h)),
        pl.BlockSpec((c, _DV), lambda b, h, j: (zr + b * nchunks + j, _Z2_OA // _DV + h)),
        pl.BlockSpec((c, LANES), lambda b, h, j: (zr + b * nchunks + j, 0)),
        pl.BlockSpec((1, SUBLANES, c), lambda b, h, j: (b, 0, j)),
        pl.BlockSpec((1, SUBLANES, _DK), lambda b, h, j: (sb(b), 0, h)),
        pl.BlockSpec((1, SUBLANES, _DK), lambda b, h, j: (sb(b), 0, H_A + h)),
        pl.BlockSpec((SUBLANES, _DK), lambda b, h, j: (0, h)),
        pl.BlockSpec((SUBLANES, _DK), lambda b, h, j: (0, H_A + h)),
        pl.BlockSpec((1, _DK), lambda b, h, j: (0, h)),
        pl.BlockSpec((1, _DK), lambda b, h, j: (0, H_A + h)),
        pl.BlockSpec((1, _DV), lambda b, h, j: (0, h)),
        pl.BlockSpec((1, 1, _DK, _DV), lambda b, h, j: (sb(b), h, 0, 0)),
        pl.BlockSpec((1, 1, 1, _DK), lambda b, h, j: (sb(b), h, 0, 0)),
        pl.BlockSpec((1, 1, SUBLANES, LANES), lambda b, h, j: (sb(b), h, 0, 0)),
    ]
    args = [z1, z1, z1, z2, gates, grow, conv0, conv0, conv_w8, conv_w8, conv_b, conv_b, mnorm, c0, n0, m0]
    aliases = {}
    if ha_prev is not None:
        in_specs.append(pl.BlockSpec(memory_space=pl.ANY))
        args.append(ha_prev)
        aliases = {len(args) - 1: 0}
    out_specs = [
        pl.BlockSpec((c, _DV), lambda b, h, j: (zr + b * nchunks + j, h)),
        pl.BlockSpec((1, 1, _DK, _DV), lambda b, h, j: (b, h, 0, 0)),
        pl.BlockSpec((1, 1, 1, _DK), lambda b, h, j: (b, h, 0, 0)),
        pl.BlockSpec((1, 1, SUBLANES, LANES), lambda b, h, j: (b, h, 0, 0)),
    ]
    out_shape = [
        jax.ShapeDtypeStruct((m_total, H_A * _DV), BF16),
        jax.ShapeDtypeStruct((nb, H_A, _DK, _DV), F32),
        jax.ShapeDtypeStruct((nb, H_A, 1, _DK), F32),
        jax.ShapeDtypeStruct((nb, H_A, SUBLANES, LANES), F32),
    ]
    return pl.pallas_call(
        functools.partial(_mlstm_body, c, _DK, nchunks),
        grid=(nb, H_A, nchunks),
        in_specs=in_specs,
        out_specs=out_specs,
        out_shape=out_shape,
        scratch_shapes=[
            pltpu.VMEM((_DK, _DV), F32),
            pltpu.VMEM((1, _DK), F32),
            pltpu.VMEM((SUBLANES, LANES), F32),
            pltpu.VMEM((c + SUBLANES, _DK), F32),
            pltpu.VMEM((c + SUBLANES, _DK), F32),
        ],
        input_output_aliases=aliases,
        compiler_params=_params(("parallel", "parallel", "arbitrary")),
        name="mlstm_scan",
    )(*args)


def _rotary(x, cos, sin):
    half = x.shape[-1] // 2
    x1 = x[:, :half]
    x2 = x[:, half:]
    return jnp.concatenate([x1 * cos - x2 * sin, x2 * cos + x1 * sin], axis=-1)


def _ret_body(c, dk, nchunks, lg_ref, q_ref, k_ref, v_ref, g_ref, cos_ref, sin_ref, rn_ref, s0_ref, *rest):
    ob_ref, sout_ref, s_s = rest[-3:]
    h = pl.program_id(1)
    j = pl.program_id(2)

    @pl.when(j == 0)
    def _():
        s_s[...] = s0_ref[0, 0]

    lg = lg_ref[h]
    cos = cos_ref[...]
    sin = sin_ref[...]
    qb = _rotary(q_ref[...], cos, sin).astype(BF16)
    kr = _rotary(k_ref[...], cos, sin) * (dk ** -0.5)
    vb = v_ref[...].astype(BF16)

    t_i = lax.broadcasted_iota(jnp.int32, (c, c), 0)
    s_i = lax.broadcasted_iota(jnp.int32, (c, c), 1)
    causal = t_i >= s_i
    diff = jnp.where(causal, t_i - s_i, 0).astype(F32)
    dmat = jnp.where(causal, jnp.exp(lg * diff), 0.0)
    t_col = lax.broadcasted_iota(jnp.int32, (c, 1), 0).astype(F32)

    scores = _dot_nt(qb, kr.astype(BF16)) * dmat
    s_old = s_s[...]
    cross = jnp.exp(lg * (t_col + 1.0))
    o = _dot(scores.astype(BF16), vb) + cross * _dot(qb, s_old.astype(BF16))
    wk = jnp.exp(lg * (c - 1.0 - t_col))
    s_s[...] = jnp.exp(lg * c) * s_old + _dot_tn((kr * wk).astype(BF16), vb)

    mu = jnp.mean(o, axis=-1, keepdims=True)
    oc = o - mu
    var = jnp.mean(oc * oc, axis=-1, keepdims=True)
    g = g_ref[...]
    ob_ref[...] = (oc * lax.rsqrt(var + EPS) * rn_ref[...] * (g * jax.nn.sigmoid(g))).astype(BF16)

    @pl.when(j == nchunks - 1)
    def _():
        sout_ref[0, 0] = s_s[...]


def _ret_scan(log_g, z2, cos, sin, rnorm, s0, *, row0, nb, t, c, share_state, ob_prev, m_total):
    nchunks = t // c
    zr = row0 // c
    sb = (lambda b: 0) if share_state else (lambda b: b)
    in_specs = [
        pl.BlockSpec((c, _DK), lambda b, h, j, lg: (zr + b * nchunks + j, _Z2_QB // _DK + h)),
        pl.BlockSpec((c, _DK), lambda b, h, j, lg: (zr + b * nchunks + j, _Z2_KB // _DK + h)),
        pl.BlockSpec((c, _DV), lambda b, h, j, lg: (zr + b * nchunks + j, _Z2_VB // _DV + h)),
        pl.BlockSpec((c, _DV), lambda b, h, j, lg: (zr + b * nchunks + j, _Z2_GB // _DV + h)),
        pl.BlockSpec((c, _DK // 2), lambda b, h, j, lg: (j, 0)),
        pl.BlockSpec((c, _DK // 2), lambda b, h, j, lg: (j, 0)),
        pl.BlockSpec((1, _DV), lambda b, h, j, lg: (0, h)),
        pl.BlockSpec((1, 1, _DK, _DV), lambda b, h, j, lg: (sb(b), h, 0, 0)),
    ]
    args = [z2, z2, z2, z2, cos, sin, rnorm, s0]
    aliases = {}
    if ob_prev is not None:
        in_specs.append(pl.BlockSpec(memory_space=pl.ANY))
        args.append(ob_prev)
        aliases = {len(args): 0}
    grid_spec = pltpu.PrefetchScalarGridSpec(
        num_scalar_prefetch=1,
        grid=(nb, H_B, nchunks),
        in_specs=in_specs,
        out_specs=[
            pl.BlockSpec((c, _DV), lambda b, h, j, lg: (zr + b * nchunks + j, h)),
            pl.BlockSpec((1, 1, _DK, _DV), lambda b, h, j, lg: (b, h, 0, 0)),
        ],
        scratch_shapes=[pltpu.VMEM((_DK, _DV), F32)],
    )
    return pl.pallas_call(
        functools.partial(_ret_body, c, _DK, nchunks),
        grid_spec=grid_spec,
        out_shape=[
            jax.ShapeDtypeStruct((m_total, H_B * _DV), BF16),
            jax.ShapeDtypeStruct((nb, H_B, _DK, _DV), F32),
        ],
        input_output_aliases=aliases,
        compiler_params=_params(("parallel", "parallel", "arbitrary")),
        name="ret_scan",
    )(log_g, *args)


def _merge_body(ha_ref, ob_ref, wa_ref, wb_ref, ga_ref, gb_ref, o_ref):
    ya = _dot(ha_ref[...], wa_ref[...])
    yb = _dot(ob_ref[...], wb_ref[...])
    o_ref[...] = (jax.nn.sigmoid(ga_ref[...]) * ya + jax.nn.sigmoid(gb_ref[...]) * yb).astype(o_ref.dtype)


def _merge(ha, ob, wa, wb, z2, tm, tn):
    m = ha.shape[0]
    d = wa.shape[1]
    return pl.pallas_call(
        _merge_body,
        grid=(d // tn, m // tm),
        in_specs=[
            pl.BlockSpec((tm, ha.shape[1]), lambda j, i: (i, 0)),
            pl.BlockSpec((tm, ob.shape[1]), lambda j, i: (i, 0)),
            pl.BlockSpec((wa.shape[0], tn), lambda j, i: (0, j)),
            pl.BlockSpec((wb.shape[0], tn), lambda j, i: (0, j)),
            pl.BlockSpec((tm, tn), lambda j, i: (i, _Z2_GA // tn + j)),
            pl.BlockSpec((tm, tn), lambda j, i: (i, _Z2_GBM // tn + j)),
        ],
        out_specs=pl.BlockSpec((tm, tn), lambda j, i: (i, j)),
        out_shape=jax.ShapeDtypeStruct((m, d), BF16),
        compiler_params=_params(("parallel", "parallel")),
        name="merge",
    )(ha, ob, wa, wb, z2, z2)


def _wo_router_body(n_first, mg_ref, wo_ref, xa_ref, xb_ref, g2_ref, wrh_ref, wrl_ref, br_ref,
                    x1_ref, h2_ref, idx_ref, wt_ref):
    def run(x_ref):
        x1 = x_ref[...] + _dot(mg_ref[...], wo_ref[...])
        x1_ref[...] = x1
        h2 = x1 * lax.rsqrt(jnp.mean(x1 * x1, axis=-1, keepdims=True) + EPS) * g2_ref[...]
        h2_ref[...] = h2
        logits = _dot3(h2, wrh_ref[...], wrl_ref[...]) + br_ref[...]
        lane = lax.broadcasted_iota(jnp.int32, logits.shape, 1)
        cur = jnp.where(lane < N_EXPERTS, logits, -jnp.inf)
        vals, idxs = [], []
        for _ in range(TOP_K):
            mx = jnp.max(cur, axis=1, keepdims=True)
            ix = jnp.min(jnp.where(cur == mx, lane, LANES), axis=1, keepdims=True)
            vals.append(mx)
            idxs.append(ix)
            cur = jnp.where(lane == ix, -jnp.inf, cur)
        exps = [jnp.exp(v - vals[0]) for v in vals]
        tot = exps[0] + exps[1] + exps[2] + exps[3]
        idx_out = jnp.zeros(logits.shape, jnp.int32)
        wt_out = jnp.zeros(logits.shape, F32)
        for kk in range(TOP_K):
            idx_out = jnp.where(lane == kk, idxs[kk], idx_out)
            wt_out = jnp.where(lane == kk, exps[kk] / tot, wt_out)
        idx_ref[...] = idx_out
        wt_ref[...] = wt_out

    i = pl.program_id(0)
    pl.when(i < n_first)(lambda: run(xa_ref))
    pl.when(i >= n_first)(lambda: run(xb_ref))


def _wo_router(merged, wo, xa, xb, g2, wr_hi, wr_lo, br, tm):
    m, d = merged.shape
    n_first = xa.shape[0] // tm
    return pl.pallas_call(
        functools.partial(_wo_router_body, n_first),
        grid=(m // tm,),
        in_specs=[
            pl.BlockSpec((tm, d), lambda i: (i, 0)),
            pl.BlockSpec((d, d), lambda i: (0, 0)),
        ] + _two_group_specs(tm, d, n_first) + [
            pl.BlockSpec((1, d), lambda i: (0, 0)),
            pl.BlockSpec((d, LANES), lambda i: (0, 0)),
            pl.BlockSpec((d, LANES), lambda i: (0, 0)),
            pl.BlockSpec((1, LANES), lambda i: (0, 0)),
        ],
        out_specs=[
            pl.BlockSpec((tm, d), lambda i: (i, 0)),
            pl.BlockSpec((tm, d), lambda i: (i, 0)),
            pl.BlockSpec((tm, LANES), lambda i: (i, 0)),
            pl.BlockSpec((tm, LANES), lambda i: (i, 0)),
        ],
        out_shape=[
            jax.ShapeDtypeStruct((m, d), F32),
            jax.ShapeDtypeStruct((m, d), F32),
            jax.ShapeDtypeStruct((m, LANES), jnp.int32),
            jax.ShapeDtypeStruct((m, LANES), F32),
        ],
        compiler_params=_params(("parallel",)),
        name="wo_router",
    )(merged, wo, xa, xb, g2, wr_hi, wr_lo, br)


def _dispatch_body(dest_ref, h_ref, xs_hbm, sem):
    tm = h_ref.shape[0]

    def row_copy(t, kk):
        return pltpu.make_async_copy(h_ref.at[pl.ds(t, 1)],
                                     xs_hbm.at[pl.ds(dest_ref[0, 0, t * TOP_K + kk], 1)], sem)

    def start(t, carry):
        for kk in range(TOP_K):
            row_copy(t, kk).start()
        return carry
    lax.fori_loop(0, tm, start, 0)

    def wait(t, carry):
        for kk in range(TOP_K):
            row_copy(t, kk).wait()
        return carry
    lax.fori_loop(0, tm, wait, 0)


def _dispatch(dest3, h2, tm):
    n, d = h2.shape
    return pl.pallas_call(
        _dispatch_body,
        grid=(n // tm,),
        in_specs=[
            pl.BlockSpec((1, 1, tm * TOP_K), lambda i: (i, 0, 0), memory_space=pltpu.SMEM),
            pl.BlockSpec((tm, d), lambda i: (i, 0)),
        ],
        out_specs=pl.BlockSpec(memory_space=pl.ANY),
        out_shape=jax.ShapeDtypeStruct((n * TOP_K, d), h2.dtype),
        scratch_shapes=[pltpu.SemaphoreType.DMA(())],
        compiler_params=_params(("arbitrary",)),
        name="moe_dispatch",
    )(dest3, h2)


def _item_flags(t, ie_ref, fb_ref, ni_ref):
    active = t < ni_ref[0]
    new_expert = jnp.logical_or(t == 0, ie_ref[t] != ie_ref[jnp.maximum(t - 1, 0)])
    return active, jnp.logical_and(active, new_expert), fb_ref[t] == 1


def _row_mask(rows, t, lo_ref, hi_ref):
    row = lax.broadcasted_iota(jnp.int32, (rows, 1), 0)
    return jnp.logical_and(row >= lo_ref[t], row < hi_ref[t])


def _stream_expert_weights(t, f, new_expert, ie_ref, grp_ref, gexp_ref, ng_ref, hbm_refs, land_refs, bf_refs, sem):
    tn = bf_refs[0].shape[1]

    def copies(e, ff):
        col = pl.multiple_of(ff * tn, tn)
        return [pltpu.make_async_copy(hbm.at[e, :, pl.ds(col, tn)], land, sem.at[k])
                for k, (hbm, land) in enumerate(zip(hbm_refs, land_refs))]

    @pl.when(jnp.logical_and(t == 0, f == 0))
    def _():
        for cp in copies(ie_ref[0], 0):
            cp.start()

    @pl.when(new_expert)
    def _():
        for cp in copies(ie_ref[t], f):
            cp.wait()
        for land, bf in zip(land_refs, bf_refs):
            bf[...] = land[...].astype(BF16)
        g = grp_ref[t]
        last = g + 1 >= ng_ref[0]
        nxt_e = gexp_ref[jnp.where(last, 0, g + 1)]
        nxt_f = jnp.where(last, f + 1, f)

        @pl.when(jnp.logical_or(jnp.logical_not(last), f + 1 < pl.num_programs(0)))
        def _():
            for cp in copies(nxt_e, nxt_f):
                cp.start()


def _expert_up_body(ib_ref, ie_ref, lo_ref, hi_ref, fb_ref, ni_ref, grp_ref, gexp_ref, ng_ref,
                    x_ref, wg_hbm, wu_hbm, bg_ref, bu_ref, a_ref, wg_l, wu_l, wg_s, wu_s, sem):
    f = pl.program_id(0)
    t = pl.program_id(1)
    active, new_expert, first_visit = _item_flags(t, ie_ref, fb_ref, ni_ref)
    _stream_expert_weights(t, f, new_expert, ie_ref, grp_ref, gexp_ref, ng_ref,
                           (wg_hbm, wu_hbm), (wg_l, wu_l), (wg_s, wu_s), sem)

    def compute():
        x = x_ref[...].astype(BF16)
        g = jnp.minimum(_dot(x, wg_s[...]) + bg_ref[0], SWIGLU_LIMIT)
        u = jnp.clip(_dot(x, wu_s[...]) + bu_ref[0], -SWIGLU_LIMIT, SWIGLU_LIMIT)
        return (g * jax.nn.sigmoid(SWIGLU_ALPHA * g) * (u + 1.0)).astype(BF16)

    mask = _row_mask(x_ref.shape[0], t, lo_ref, hi_ref)

    @pl.when(jnp.logical_and(active, first_visit))
    def _():
        a_ref[...] = jnp.where(mask, compute(), jnp.zeros(a_ref.shape, BF16))

    @pl.when(jnp.logical_and(active, jnp.logical_not(first_visit)))
    def _():
        a_ref[...] = jnp.where(mask, compute(), a_ref[...])


def _item_spec(shape, fn):
    def index_map(f, t, ib, ie, lo, hi, fb, ni, *_):
        tc = jnp.minimum(t, ni[0] - 1)
        return fn(tc, f, ib, ie)
    return pl.BlockSpec(shape, index_map)


_N_TABLES = 9


def _expert_up(tables, xs, w_gate, w_up, b_gate, b_up, rows, tf):
    p, d = xs.shape
    dff = w_gate.shape[2]
    n_items = tables[0].shape[0]
    grid_spec = pltpu.PrefetchScalarGridSpec(
        num_scalar_prefetch=_N_TABLES,
        grid=(dff // tf, n_items),
        in_specs=[
            _item_spec((rows, d), lambda t, f, ib, ie: (ib[t], 0)),
            pl.BlockSpec(memory_space=pl.ANY),
            pl.BlockSpec(memory_space=pl.ANY),
            _item_spec((1, 1, tf), lambda t, f, ib, ie: (ie[t], 0, f)),
            _item_spec((1, 1, tf), lambda t, f, ib, ie: (ie[t], 0, f)),
        ],
        out_specs=_item_spec((rows, tf), lambda t, f, ib, ie: (ib[t], f)),
        scratch_shapes=[pltpu.VMEM((d, tf), F32), pltpu.VMEM((d, tf), F32),
                        pltpu.VMEM((d, tf), BF16), pltpu.VMEM((d, tf), BF16),
                        pltpu.SemaphoreType.DMA((2,))],
    )
    return pl.pallas_call(
        _expert_up_body,
        grid_spec=grid_spec,
        out_shape=jax.ShapeDtypeStruct((p, dff), BF16),
        compiler_params=_params(("arbitrary", "arbitrary")),
        name="expert_up",
    )(*tables, xs, w_gate, w_up, b_gate, b_up)


def _expert_down_body(ib_ref, ie_ref, lo_ref, hi_ref, fb_ref, ni_ref, grp_ref, gexp_ref, ng_ref,
                      a_ref, wd_hbm, bd_ref, o_ref, wd_l, wd_s, sem):
    f = pl.program_id(0)
    t = pl.program_id(1)
    active, new_expert, first_visit = _item_flags(t, ie_ref, fb_ref, ni_ref)
    _stream_expert_weights(t, f, new_expert, ie_ref, grp_ref, gexp_ref, ng_ref,
                           (wd_hbm,), (wd_l,), (wd_s,), sem)

    def compute():
        return _dot(a_ref[...], wd_s[...]) + bd_ref[0]

    mask = _row_mask(a_ref.shape[0], t, lo_ref, hi_ref)

    @pl.when(jnp.logical_and(active, first_visit))
    def _():
        o_ref[...] = jnp.where(mask, compute(), 0.0)

    @pl.when(jnp.logical_and(active, jnp.logical_not(first_visit)))
    def _():
        o_ref[...] = jnp.where(mask, compute(), o_ref[...])


def _expert_down(tables, a, w_down, b_down, rows, tn):
    p, dff = a.shape
    d = w_down.shape[2]
    n_items = tables[0].shape[0]
    grid_spec = pltpu.PrefetchScalarGridSpec(
        num_scalar_prefetch=_N_TABLES,
        grid=(d // tn, n_items),
        in_specs=[
            _item_spec((rows, dff), lambda t, f, ib, ie: (ib[t], 0)),
            pl.BlockSpec(memory_space=pl.ANY),
            _item_spec((1, 1, tn), lambda t, f, ib, ie: (ie[t], 0, f)),
        ],
        out_specs=_item_spec((rows, tn), lambda t, f, ib, ie: (ib[t], f)),
        scratch_shapes=[pltpu.VMEM((dff, tn), F32), pltpu.VMEM((dff, tn), BF16),
                        pltpu.SemaphoreType.DMA((1,))],
    )
    return pl.pallas_call(
        _expert_down_body,
        grid_spec=grid_spec,
        out_shape=jax.ShapeDtypeStruct((p, d), F32),
        compiler_params=_params(("arbitrary", "arbitrary")),
        name="expert_down",
    )(*tables, a, w_down, b_down)


def _combine_body(dest_ref, wt_ref, x1_ref, fn_ref, rows_hbm, y_ref, buf, sem):
    tm = x1_ref.shape[0]

    def row_copy(t, kk):
        return pltpu.make_async_copy(rows_hbm.at[pl.ds(dest_ref[0, 0, t * TOP_K + kk], 1)],
                                     buf.at[kk, pl.ds(t, 1)], sem)

    def start(t, carry):
        for kk in range(TOP_K):
            row_copy(t, kk).start()
        return carry
    lax.fori_loop(0, tm, start, 0)

    def wait(t, carry):
        for kk in range(TOP_K):
            row_copy(t, kk).wait()
        return carry
    lax.fori_loop(0, tm, wait, 0)

    wt = wt_ref[...]
    moe = wt[:, 0:1] * buf[0]
    for kk in range(1, TOP_K):
        moe = moe + wt[:, kk:kk + 1] * buf[kk]
    y = x1_ref[...] + moe
    y_ref[...] = y * lax.rsqrt(jnp.mean(y * y, axis=-1, keepdims=True) + EPS) * fn_ref[...]


def _combine(dest3, wts, x1, final_norm, out_rows, *, tile0, ntiles, tm):
    d = x1.shape[1]
    return pl.pallas_call(
        _combine_body,
        grid=(ntiles,),
        in_specs=[
            pl.BlockSpec((1, 1, tm * TOP_K), lambda i: (tile0 + i, 0, 0), memory_space=pltpu.SMEM),
            pl.BlockSpec((tm, LANES), lambda i: (tile0 + i, 0)),
            pl.BlockSpec((tm, d), lambda i: (tile0 + i, 0)),
            pl.BlockSpec((1, d), lambda i: (0, 0)),
            pl.BlockSpec(memory_space=pl.ANY),
        ],
        out_specs=pl.BlockSpec((tm, d), lambda i: (i, 0)),
        out_shape=jax.ShapeDtypeStruct((ntiles * tm, d), F32),
        scratch_shapes=[pltpu.VMEM((TOP_K, tm, d), F32), pltpu.SemaphoreType.DMA(())],
        compiler_params=_params(("arbitrary",)),
        name="moe_combine",
    )(dest3, wts, x1, final_norm, out_rows)


def _routing_tables(top_idx, rows):
    n = top_idx.shape[0]
    m = n * TOP_K
    experts = jnp.arange(N_EXPERTS, dtype=jnp.int32)
    e_flat = top_idx.reshape(m)
    onehot = (e_flat[:, None] == experts[None, :]).astype(jnp.int32)
    csum = jnp.cumsum(onehot, axis=0)
    counts = csum[-1]
    ends = jnp.cumsum(counts)
    starts = ends - counts
    dest = jnp.sum(onehot * (csum - 1 + starts[None, :]), axis=1)

    first_blk = starts // rows
    last_blk = (ends - 1) // rows
    n_it = jnp.where(counts > 0, last_blk - first_blk + 1, 0)
    it_end = jnp.cumsum(n_it)
    it_start = it_end - n_it
    n_items = it_end[-1:]
    max_items = m // rows + N_EXPERTS
    t = jnp.minimum(jnp.arange(max_items, dtype=jnp.int32), n_items[0] - 1)
    ie = jnp.sum((it_end[None, :] <= t[:, None]).astype(jnp.int32), axis=1)
    sel = (ie[:, None] == experts[None, :]).astype(jnp.int32)
    pick = lambda v: jnp.sum(sel * v[None, :], axis=1)
    ib = pick(first_blk) + t - pick(it_start)
    lo = jnp.clip(pick(starts) - ib * rows, 0, rows)
    hi = jnp.clip(pick(ends) - ib * rows, 0, rows)
    fb = jnp.concatenate([jnp.ones((1,), jnp.int32), (ib[1:] != ib[:-1]).astype(jnp.int32)])
    has = (counts > 0).astype(jnp.int32)
    pos = jnp.cumsum(has) - 1
    grp = pick(pos)
    gexp = jnp.sum(((pos[None, :] == experts[:, None]) & (has[None, :] == 1)).astype(jnp.int32) * experts[None, :],
                   axis=1)
    n_groups = jnp.sum(has, keepdims=True)
    i32 = lambda v: v.astype(jnp.int32)
    return i32(dest), (i32(ib), i32(ie), i32(lo), i32(hi), fb, i32(n_items), i32(grp), i32(gexp), i32(n_groups))


def _pad_rows8(a, at):
    b, r, n = a.shape
    return jnp.zeros((b, SUBLANES, n), a.dtype).at[:, at:at + r].set(a)


def _rope_tables(pos, half):
    inv = jnp.power(ROPE_BASE, -jnp.arange(half, dtype=F32) / half)
    ang = pos.astype(F32)[:, None] * inv[None, :]
    return jnp.cos(ang), jnp.sin(ang)


def kernel(x_prompt, x_sample, state_mlstm_C, state_mlstm_n, state_mlstm_m, state_mlstm_conv, state_ret,
           meta_tokens, norm1, w_in, b_igate, b_fgate, conv_w, conv_b, mlstm_norm, ret_norm,
           w_out_a, w_out_b, w_o, norm2, w_router, b_router, w_gate, b_gate, w_up, b_up,
           w_down, b_down, final_norm):
    bp, tp, d = x_prompt.shape
    bs, ts, _ = x_sample.shape
    assert norm1.shape[0] == 1, "single-layer stack"
    n_p, n_s = bp * tp, bs * ts
    n_tok = n_p + n_s
    qk_cols = 2 * H_A * _DK
    gate_col0 = qk_cols + H_A * _DV

    w_in0 = w_in[0]
    rest_col0 = gate_col0 + 2 * H_A
    rest_cols = w_in0.shape[1] - rest_col0
    w_if =jnp.zeros((d, LANES), F32).at[:, :2 * H_A].set(w_in0[:, gate_col0:gate_col0 + 2 * H_A])
    w_if_hi, w_if_lo = _split_bf16(w_if)
    b_if = jnp.zeros((1, LANES), F32).at[0, :H_A].set(b_igate[0]).at[0, H_A:2 * H_A].set(b_fgate[0])
    w_r = jnp.zeros((d, LANES), F32).at[:, :N_EXPERTS].set(w_router[0])
    w_r_hi, w_r_lo = _split_bf16(w_r)
    b_r = jnp.zeros((1, LANES), F32).at[0, :N_EXPERTS].set(b_router[0])
    conv_w8 = jnp.zeros((SUBLANES, qk_cols), F32).at[:CONV_W].set(conv_w[0])
    conv_b1 = conv_b[0].reshape(1, qk_cols)
    g1 = norm1[0].reshape(1, d)
    g2 = norm2[0].reshape(1, d)
    gf = final_norm.reshape(1, d)
    mnorm = mlstm_norm[0].reshape(1, H_A * _DV)
    rnorm = ret_norm[0].reshape(1, H_B * _DV)
    log_g = jnp.asarray(np.log1p(-np.power(2.0, -5.0 - np.arange(H_B, dtype=np.float32))).astype(np.float32))

    xp2 = x_prompt.reshape(n_p, d)
    xs2 = x_sample.reshape(n_s, d)
    tm = next(t for t in (512, 256, 128) if n_p % t == 0 and n_s % t == 0)
    h_all, gates = _norm_gates(xp2, xs2, g1, w_if_hi, w_if_lo, b_if, tm)
    z1 = _in_proj(h_all, w_in0, tm, 1024, 0, gate_col0)
    z2 = _in_proj(h_all, w_in0, tm, 1024, rest_col0, rest_cols)
    h_meta, gates_meta = _norm_gates(meta_tokens, None, g1, w_if_hi, w_if_lo, b_if, N_META)
    z1_meta = _in_proj(h_meta, w_in0, N_META, 1024, 0, gate_col0)
    z2_meta = _in_proj(h_meta, w_in0, N_META, 1024, rest_col0, rest_cols)

    def gate_rows(g, nb, t):
        return jnp.transpose(g[:, :SUBLANES].reshape(nb, t, SUBLANES), (0, 2, 1))

    zeros_c = jnp.zeros((1, H_A, _DK, _DV), F32)
    zeros_n = jnp.zeros((1, H_A, 1, _DK), F32)
    zeros_m = jnp.zeros((1, H_A, SUBLANES, LANES), F32)
    zeros_conv = jnp.zeros((1, SUBLANES, qk_cols), F32)
    _, c_meta, n_meta, m_meta = _mlstm_scan(
        z1_meta, z2_meta, gates_meta, gate_rows(gates_meta, 1, N_META), zeros_conv, conv_w8, conv_b1, mnorm,
        zeros_c, zeros_n, zeros_m, row0=0, nb=1, t=N_META, c=N_META,
        share_state=False, ha_prev=None, m_total=N_META)
    conv_meta = _pad_rows8(z1_meta[None, N_META - (CONV_W - 1):, :qk_cols], SUBLANES - (CONV_W - 1))
    cp = PROMPT_CHUNK if tp % PROMPT_CHUNK == 0 else tp
    ha, p_c, p_n, p_m = _mlstm_scan(
        z1, z2, gates, gate_rows(gates[:n_p], bp, tp), conv_meta, conv_w8, conv_b1, mnorm,
        c_meta, n_meta, m_meta, row0=0, nb=bp, t=tp, c=cp,
        share_state=True, ha_prev=None, m_total=n_tok)
    conv_s0 = _pad_rows8(state_mlstm_conv[0], SUBLANES - (CONV_W - 1))
    m_s0 = jnp.broadcast_to(state_mlstm_m[0][:, :, None, None], (bs, H_A, SUBLANES, LANES))
    ha, s_c, s_n, s_m = _mlstm_scan(
        z1, z2, gates, gate_rows(gates[n_p:], bs, ts), conv_s0, conv_w8, conv_b1, mnorm,
        state_mlstm_C[0], state_mlstm_n[0].reshape(bs, H_A, 1, _DK), m_s0,
        row0=n_p, nb=bs, t=ts, c=ts, share_state=False, ha_prev=ha, m_total=n_tok)

    cos_m, sin_m = _rope_tables(jnp.arange(N_META, dtype=jnp.int32), _DK // 2)
    cos_p, sin_p = _rope_tables(N_META + jnp.arange(tp, dtype=jnp.int32), _DK // 2)
    cos_s, sin_s = _rope_tables(N_META + PAST_LEN + jnp.arange(ts, dtype=jnp.int32), _DK // 2)
    zeros_s = jnp.zeros((1, H_B, _DK, _DV), F32)
    _, s_meta = _ret_scan(log_g, z2_meta, cos_m, sin_m, rnorm, zeros_s, row0=0, nb=1,
                          t=N_META, c=N_META, share_state=False, ob_prev=None, m_total=N_META)
    ob, p_ret = _ret_scan(log_g, z2, cos_p, sin_p, rnorm, s_meta, row0=0, nb=bp, t=tp, c=cp,
                          share_state=True, ob_prev=None, m_total=n_tok)
    ob, s_ret = _ret_scan(log_g, z2, cos_s, sin_s, rnorm, state_ret[0], row0=n_p, nb=bs,
                          t=ts, c=ts, share_state=False, ob_prev=ob, m_total=n_tok)

    merged = _merge(ha, ob, w_out_a[0].astype(BF16), w_out_b[0].astype(BF16), z2, tm, 1024)
    x1, h2, top_idx, wts = _wo_router(merged, w_o[0].astype(BF16), xp2, xs2, g2, w_r_hi, w_r_lo, b_r,
                                      min(tm, 256))

    rows = EXPERT_ROWS
    dest, tables = _routing_tables(top_idx[:, :TOP_K], rows)
    td = DISPATCH_ROWS
    dest3 = dest.reshape(n_tok // td, 1, td * TOP_K)
    xs = _dispatch(dest3, h2, td)
    dff = w_gate.shape[3]
    act = _expert_up(tables, xs, w_gate[0], w_up[0],
                     b_gate[0].reshape(N_EXPERTS, 1, dff), b_up[0].reshape(N_EXPERTS, 1, dff), rows, 1024)
    out_rows = _expert_down(tables, act, w_down[0], b_down[0].reshape(N_EXPERTS, 1, d), rows, 1024)
    y_p = _combine(dest3, wts, x1, gf, out_rows, tile0=0, ntiles=n_p // td, tm=td)
    y_s = _combine(dest3, wts, x1, gf, out_rows, tile0=n_p // td, ntiles=n_s // td, tm=td)

    def tail_rows(row0, nb, t):
        return jnp.stack([z1[row0 + (b + 1) * t - (CONV_W - 1):row0 + (b + 1) * t, :qk_cols] for b in range(nb)])

    p_conv = tail_rows(0, bp, tp)
    s_conv = tail_rows(n_p, bs, ts)
    return (y_p.reshape(bp, tp, d), y_s.reshape(bs, ts, d),
            p_c[None], p_n.reshape(1, bp, H_A, _DK), p_m[None, :, :, 0, 0], p_conv[None], p_ret[None],
            s_c[None], s_n.reshape(1, bs, H_A, _DK), s_m[None, :, :, 0, 0], s_conv[None], s_ret[None])
```

```python
import functools

import numpy as np
import jax
import jax.numpy as jnp
from jax import lax
from jax.experimental import pallas as pl
from jax.experimental.pallas import tpu as pltpu

F32 = jnp.float32
BF16 = jnp.bfloat16

EPS = 1e-5
N_META = 16
PAST_LEN = 1024
H_A = 4
H_B = 8
CONV_W = 4
ROPE_BASE = 10000.0
N_EXPERTS = 32
TOP_K = 4
SWIGLU_LIMIT = 7.0
SWIGLU_ALPHA = 1.702

LANES = 128
SUBLANES = 8
VMEM_LIMIT = 56 * 1024 * 1024
PROMPT_CHUNK = 512
EXPERT_ROWS = 256
DISPATCH_ROWS = 256


def _params(sem, vmem=VMEM_LIMIT):
    return pltpu.CompilerParams(dimension_semantics=sem, vmem_limit_bytes=vmem)


def _dot(a, b):
    return jnp.dot(a, b, preferred_element_type=F32)


def _dot_nt(a, b):
    return lax.dot_general(a, b, (((1,), (1,)), ((), ())), preferred_element_type=F32)


def _dot_tn(a, b):
    return lax.dot_general(a, b, (((0,), (0,)), ((), ())), preferred_element_type=F32)


def _split_bf16(w):
    hi = w.astype(BF16)
    lo = (w - hi.astype(F32)).astype(BF16)
    return hi, lo


def _dot3(a, w_hi, w_lo):
    a_hi = a.astype(BF16)
    a_lo = (a - a_hi.astype(F32)).astype(BF16)
    return _dot(a_hi, w_hi) + _dot(a_hi, w_lo) + _dot(a_lo, w_hi)


def _two_group_specs(tm, d, n_first):
    return [
        pl.BlockSpec((tm, d), lambda i: (jnp.minimum(i, n_first - 1), 0)),
        pl.BlockSpec((tm, d), lambda i: (jnp.maximum(i - n_first, 0), 0)),
    ]


def _norm_gates_body(n_first, xa_ref, xb_ref, g_ref, whi_ref, wlo_ref, b_ref, h_ref, gates_ref):
    def run(x_ref):
        x = x_ref[...]
        y = x * lax.rsqrt(jnp.mean(x * x, axis=-1, keepdims=True) + EPS) * g_ref[...]
        h_ref[...] = y.astype(BF16)
        pre = _dot3(y, whi_ref[...], wlo_ref[...]) + b_ref[...]
        lane = lax.broadcasted_iota(jnp.int32, pre.shape, 1)
        logsig = jnp.minimum(pre, 0.0) - jnp.log1p(jnp.exp(-jnp.abs(pre)))
        gates_ref[...] = jnp.where(lane < H_A, pre, logsig)

    i = pl.program_id(0)
    pl.when(i < n_first)(lambda: run(xa_ref))
    pl.when(i >= n_first)(lambda: run(xb_ref))


def _norm_gates(xa, xb, gain, w_hi, w_lo, bias, tm):
    d = xa.shape[1]
    n_first = xa.shape[0] // tm
    m = xa.shape[0] + (0 if xb is None else xb.shape[0])
    xb = xa if xb is None else xb
    return pl.pallas_call(
        functools.partial(_norm_gates_body, n_first),
        grid=(m // tm,),
        in_specs=_two_group_specs(tm, d, n_first) + [
            pl.BlockSpec((1, d), lambda i: (0, 0)),
            pl.BlockSpec((d, LANES), lambda i: (0, 0)),
            pl.BlockSpec((d, LANES), lambda i: (0, 0)),
            pl.BlockSpec((1, LANES), lambda i: (0, 0)),
        ],
        out_specs=[
            pl.BlockSpec((tm, d), lambda i: (i, 0)),
            pl.BlockSpec((tm, LANES), lambda i: (i, 0)),
        ],
        out_shape=[
            jax.ShapeDtypeStruct((m, d), BF16),
            jax.ShapeDtypeStruct((m, LANES), F32),
        ],
        compiler_params=_params(("parallel",)),
        name="norm_gates",
    )(xa, xb, gain, w_hi, w_lo, bias)


_CAST_ROWS = 256


def _mm_cast_body(shift, x_ref, w_ref, wn_ref, o_ref, w_s):
    @pl.when(pl.program_id(1) == 0)
    def _():
        k, tn = w_s.shape
        for r0 in range(0, k, _CAST_ROWS):
            rows = slice(r0, r0 + _CAST_ROWS)
            if shift:
                both = jnp.concatenate([w_ref[rows, :], wn_ref[rows, :]], axis=1)
                w_s[rows, :] = both[:, shift:shift + tn].astype(BF16)
            else:
                w_s[rows, :] = w_ref[rows, :].astype(BF16)
    o_ref[...] = _dot(x_ref[...], w_s[...]).astype(o_ref.dtype)


def _in_proj(h, w, tm, tn, col0, ncols):
    m, k = h.shape
    shift = col0 % LANES
    base = col0 - shift
    assert base % tn == 0 and ncols % tn == 0 and k % _CAST_ROWS == 0
    return pl.pallas_call(
        functools.partial(_mm_cast_body, shift),
        grid=(ncols // tn, m // tm),
        in_specs=[
            pl.BlockSpec((tm, k), lambda j, i: (i, 0)),
            pl.BlockSpec((k, tn), lambda j, i: (0, base // tn + j)),
            pl.BlockSpec((k, LANES), lambda j, i: (0, (base + (j + 1) * tn) // LANES)),
        ],
        out_specs=pl.BlockSpec((tm, tn), lambda j, i: (i, j)),
        out_shape=jax.ShapeDtypeStruct((m, ncols), F32),
        scratch_shapes=[pltpu.VMEM((k, tn), BF16)],
        compiler_params=_params(("parallel", "arbitrary")),
        name="in_proj",
    )(h, w, w)


def _causal_conv_silu(buf, x_ref, w_ref, b_ref, c):
    buf[SUBLANES:SUBLANES + c, :] = x_ref[...]
    acc = b_ref[...] + w_ref[0:1, :] * buf[5:5 + c, :]
    acc = acc + w_ref[1:2, :] * buf[6:6 + c, :]
    acc = acc + w_ref[2:3, :] * buf[7:7 + c, :]
    acc = acc + w_ref[3:4, :] * buf[8:8 + c, :]
    buf[0:SUBLANES, :] = buf[c:c + SUBLANES, :]
    return acc * jax.nn.sigmoid(acc)


def _mlstm_body(c, dk, nchunks, qpre_ref, kpre_ref, v_ref, oa_ref, gcol_ref, grow_ref,
                cq0_ref, ck0_ref, cwq_ref, cwk_ref, cbq_ref, cbk_ref, mn_ref,
                c0_ref, n0_ref, m0_ref, *rest):
    ha_ref, cout_ref, nout_ref, mout_ref, c_s, n_s, m_s, qbuf, kbuf = rest[-9:]
    h = pl.program_id(1)
    j = pl.program_id(2)

    @pl.when(j == 0)
    def _():
        c_s[...] = c0_ref[0, 0]
        n_s[...] = n0_ref[0, 0]
        m_s[...] = m0_ref[0, 0]
        qbuf[0:SUBLANES, :] = cq0_ref[0]
        kbuf[0:SUBLANES, :] = ck0_ref[0]

    q = _causal_conv_silu(qbuf, qpre_ref, cwq_ref, cbq_ref, c)
    k = _causal_conv_silu(kbuf, kpre_ref, cwk_ref, cbk_ref, c) * (dk ** -0.5)

    gc = gcol_ref[...]
    lane = lax.broadcasted_iota(jnp.int32, gc.shape, 1)
    ig_col = jnp.sum(jnp.where(lane == h, gc, 0.0), axis=1, keepdims=True)
    lf_col = jnp.sum(jnp.where(lane == h + H_A, gc, 0.0), axis=1, keepdims=True)
    gr = grow_ref[0]
    sub = lax.broadcasted_iota(jnp.int32, gr.shape, 0)
    ig_row = jnp.sum(jnp.where(sub == h, gr, 0.0), axis=0, keepdims=True)
    lf_row = jnp.sum(jnp.where(sub == h + H_A, gr, 0.0), axis=0, keepdims=True)

    t_i = lax.broadcasted_iota(jnp.int32, (c, c), 0)
    s_i = lax.broadcasted_iota(jnp.int32, (c, c), 1)
    causal = s_i <= t_i
    b_col = jnp.sum(jnp.where(causal, lf_row, 0.0), axis=1, keepdims=True)
    b_row = jnp.sum(jnp.where(t_i <= s_i, lf_col, 0.0), axis=0, keepdims=True)
    m_prev = m_s[0:1, 0:1]
    log_d = jnp.where(causal, b_col - b_row + ig_row, -jnp.inf)
    inter = b_col + m_prev
    m_tok = jnp.maximum(inter, jnp.max(log_d, axis=1, keepdims=True))
    dw = jnp.exp(log_d - m_tok)
    a_inter = jnp.exp(inter - m_tok)

    qb = q.astype(BF16)
    kb = k.astype(BF16)
    vb = v_ref[...].astype(BF16)
    s = _dot_nt(qb, kb) * dw
    c_old = c_s[...]
    num = a_inter * _dot(qb, c_old.astype(BF16)) + _dot(s.astype(BF16), vb)
    den = a_inter * jnp.sum(q * n_s[...], axis=1, keepdims=True) + jnp.sum(s, axis=1, keepdims=True)
    den = jnp.maximum(jnp.abs(den), jnp.exp(-m_tok))
    hh = num * (1.0 / den)

    b_last = jnp.sum(lf_row, axis=1, keepdims=True)
    logw = b_last - b_col + ig_col
    m_new = jnp.maximum(b_last + m_prev, jnp.max(logw, axis=0, keepdims=True))
    kw = k * jnp.exp(logw - m_new)
    decay = jnp.exp(b_last + m_prev - m_new)
    c_s[...] = decay * c_old + _dot_tn(kw.astype(BF16), vb)
    n_s[...] = decay * n_s[...] + jnp.sum(kw, axis=0, keepdims=True)
    m_s[...] = jnp.broadcast_to(m_new, m_s.shape)

    hn = hh * lax.rsqrt(jnp.mean(hh * hh, axis=-1, keepdims=True) + EPS) * mn_ref[...]
    ha_ref[...] = (hn * jax.nn.sigmoid(oa_ref[...])).astype(BF16)

    @pl.when(j == nchunks - 1)
    def _():
        cout_ref[0, 0] = c_s[...]
        nout_ref[0, 0] = n_s[...]
        mout_ref[0, 0] = m_s[...]


_DK = 256
_DV = 512
_Z1_QK, _Z1_VA = 0, 2048
_Z2_OA, _Z2_QB, _Z2_KB, _Z2_VB, _Z2_GB, _Z2_GA, _Z2_GBM = 0, 2048, 4096, 6144, 10240, 14336, 16384


def _mlstm_scan(z1, z2, gates, grow, conv0, conv_w8, conv_b, mnorm, c0, n0, m0, *, row0, nb, t, c,
                share_state, ha_prev, m_total):
    nchunks = t // c
    zr = row0 // c
    sb = (lambda b: 0) if share_state else (lambda b: b)
    in_specs = [
        pl.BlockSpec((c, _DK), lambda b, h, j: (zr + b * nchunks + j, _Z1_QK // _DK + h)),
        pl.BlockSpec((c, _DK), lambda b, h, j: (zr + b * nchunks + j, _Z1_QK // _DK + H_A + h)),
        pl.BlockSpec((c, _DV), lambda b, h, j: (zr + b * nchunks + j, _Z1_VA // _DV + h)),
        pl.BlockSpec((c, _DV), lambda b, h, j: (zr + b * nchunks + j, _Z2_OA // _DV + h)),
        pl.BlockSpec((c, LANES), lambda b, h, j: (zr + b * nchunks + j, 0)),
        pl.BlockSpec((1, SUBLANES, c), lambda b, h, j: (b, 0, j)),
        pl.BlockSpec((1, SUBLANES, _DK), lambda b, h, j: (sb(b), 0, h)),
        pl.BlockSpec((1, SUBLANES, _DK), lambda b, h, j: (sb(b), 0, H_A + h)),
        pl.BlockSpec((SUBLANES, _DK), lambda b, h, j: (0, h)),
        pl.BlockSpec((SUBLANES, _DK), lambda b, h, j: (0, H_A + h)),
        pl.BlockSpec((1, _DK), lambda b, h, j: (0, h)),
        pl.BlockSpec((1, _DK), lambda b, h, j: (0, H_A + h)),
        pl.BlockSpec((1, _DV), lambda b, h, j: (0, h)),
        pl.BlockSpec((1, 1, _DK, _DV), lambda b, h, j: (sb(b), h, 0, 0)),
        pl.BlockSpec((1, 1, 1, _DK), lambda b, h, j: (sb(b), h, 0, 0)),
        pl.BlockSpec((1, 1, SUBLANES, LANES), lambda b, h, j: (sb(b), h, 0, 0)),
    ]
    args = [z1, z1, z1, z2, gates, grow, conv0, conv0, conv_w8, conv_w8, conv_b, conv_b, mnorm, c0, n0, m0]
    aliases = {}
    if ha_prev is not None:
        in_specs.append(pl.BlockSpec(memory_space=pl.ANY))
        args.append(ha_prev)
        aliases = {len(args) - 1: 0}
    out_specs = [
        pl.BlockSpec((c, _DV), lambda b, h, j: (zr + b * nchunks + j, h)),
        pl.BlockSpec((1, 1, _DK, _DV), lambda b, h, j: (b, h, 0, 0)),
        pl.BlockSpec((1, 1, 1, _DK), lambda b, h, j: (b, h, 0, 0)),
        pl.BlockSpec((1, 1, SUBLANES, LANES), lambda b, h, j: (b, h, 0, 0)),
    ]
    out_shape = [
        jax.ShapeDtypeStruct((m_total, H_A * _DV), BF16),
        jax.ShapeDtypeStruct((nb, H_A, _DK, _DV), F32),
        jax.ShapeDtypeStruct((nb, H_A, 1, _DK), F32),
        jax.ShapeDtypeStruct((nb, H_A, SUBLANES, LANES), F32),
    ]
    return pl.pallas_call(
        functools.partial(_mlstm_body, c, _DK, nchunks),
        grid=(nb, H_A, nchunks),
        in_specs=in_specs,
        out_specs=out_specs,
        out_shape=out_shape,
        scratch_shapes=[
            pltpu.VMEM((_DK, _DV), F32),
            pltpu.VMEM((1, _DK), F32),
            pltpu.VMEM((SUBLANES, LANES), F32),
            pltpu.VMEM((c + SUBLANES, _DK), F32),
            pltpu.VMEM((c + SUBLANES, _DK), F32),
        ],
        input_output_aliases=aliases,
        compiler_params=_params(("parallel", "parallel", "arbitrary")),
        name="mlstm_scan",
    )(*args)


def _rotary(x, cos, sin):
    half = x.shape[-1] // 2
    x1 = x[:, :half]
    x2 = x[:, half:]
    return jnp.concatenate([x1 * cos - x2 * sin, x2 * cos + x1 * sin], axis=-1)


def _ret_body(c, dk, nchunks, lg_ref, q_ref, k_ref, v_ref, g_ref, cos_ref, sin_ref, rn_ref, s0_ref, *rest):
    ob_ref, sout_ref, s_s = rest[-3:]
    h = pl.program_id(1)
    j = pl.program_id(2)

    @pl.when(j == 0)
    def _():
        s_s[...] = s0_ref[0, 0]

    lg = lg_ref[h]
    cos = cos_ref[...]
    sin = sin_ref[...]
    qb = _rotary(q_ref[...], cos, sin).astype(BF16)
    kr = _rotary(k_ref[...], cos, sin) * (dk ** -0.5)
    vb = v_ref[...].astype(BF16)

    t_i = lax.broadcasted_iota(jnp.int32, (c, c), 0)
    s_i = lax.broadcasted_iota(jnp.int32, (c, c), 1)
    causal = t_i >= s_i
    diff = jnp.where(causal, t_i - s_i, 0).astype(F32)
    dmat = jnp.where(causal, jnp.exp(lg * diff), 0.0)
    t_col = lax.broadcasted_iota(jnp.int32, (c, 1), 0).astype(F32)

    scores = _dot_nt(qb, kr.astype(BF16)) * dmat
    s_old = s_s[...]
    cross = jnp.exp(lg * (t_col + 1.0))
    o = _dot(scores.astype(BF16), vb) + cross * _dot(qb, s_old.astype(BF16))
    wk = jnp.exp(lg * (c - 1.0 - t_col))
    s_s[...] = jnp.exp(lg * c) * s_old + _dot_tn((kr * wk).astype(BF16), vb)

    mu = jnp.mean(o, axis=-1, keepdims=True)
    oc = o - mu
    var = jnp.mean(oc * oc, axis=-1, keepdims=True)
    g = g_ref[...]
    ob_ref[...] = (oc * lax.rsqrt(var + EPS) * rn_ref[...] * (g * jax.nn.sigmoid(g))).astype(BF16)

    @pl.when(j == nchunks - 1)
    def _():
        sout_ref[0, 0] = s_s[...]


def _ret_scan(log_g, z2, cos, sin, rnorm, s0, *, row0, nb, t, c, share_state, ob_prev, m_total):
    nchunks = t // c
    zr = row0 // c
    sb = (lambda b: 0) if share_state else (lambda b: b)
    in_specs = [
        pl.BlockSpec((c, _DK), lambda b, h, j, lg: (zr + b * nchunks + j, _Z2_QB // _DK + h)),
        pl.BlockSpec((c, _DK), lambda b, h, j, lg: (zr + b * nchunks + j, _Z2_KB // _DK + h)),
        pl.BlockSpec((c, _DV), lambda b, h, j, lg: (zr + b * nchunks + j, _Z2_VB // _DV + h)),
        pl.BlockSpec((c, _DV), lambda b, h, j, lg: (zr + b * nchunks + j, _Z2_GB // _DV + h)),
        pl.BlockSpec((c, _DK // 2), lambda b, h, j, lg: (j, 0)),
        pl.BlockSpec((c, _DK // 2), lambda b, h, j, lg: (j, 0)),
        pl.BlockSpec((1, _DV), lambda b, h, j, lg: (0, h)),
        pl.BlockSpec((1, 1, _DK, _DV), lambda b, h, j, lg: (sb(b), h, 0, 0)),
    ]
    args = [z2, z2, z2, z2, cos, sin, rnorm, s0]
    aliases = {}
    if ob_prev is not None:
        in_specs.append(pl.BlockSpec(memory_space=pl.ANY))
        args.append(ob_prev)
        aliases = {len(args): 0}
    grid_spec = pltpu.PrefetchScalarGridSpec(
        num_scalar_prefetch=1,
        grid=(nb, H_B, nchunks),
        in_specs=in_specs,
        out_specs=[
            pl.BlockSpec((c, _DV), lambda b, h, j, lg: (zr + b * nchunks + j, h)),
            pl.BlockSpec((1, 1, _DK, _DV), lambda b, h, j, lg: (b, h, 0, 0)),
        ],
        scratch_shapes=[pltpu.VMEM((_DK, _DV), F32)],
    )
    return pl.pallas_call(
        functools.partial(_ret_body, c, _DK, nchunks),
        grid_spec=grid_spec,
        out_shape=[
            jax.ShapeDtypeStruct((m_total, H_B * _DV), BF16),
            jax.ShapeDtypeStruct((nb, H_B, _DK, _DV), F32),
        ],
        input_output_aliases=aliases,
        compiler_params=_params(("parallel", "parallel", "arbitrary")),
        name="ret_scan",
    )(log_g, *args)


def _merge_body(ha_ref, ob_ref, wa_ref, wb_ref, ga_ref, gb_ref, o_ref):
    ya = _dot(ha_ref[...], wa_ref[...])
    yb = _dot(ob_ref[...], wb_ref[...])
    o_ref[...] = (jax.nn.sigmoid(ga_ref[...]) * ya + jax.nn.sigmoid(gb_ref[...]) * yb).astype(o_ref.dtype)


def _merge(ha, ob, wa, wb, z2, tm, tn):
    m = ha.shape[0]
    d = wa.shape[1]
    return pl.pallas_call(
        _merge_body,
        grid=(d // tn, m // tm),
        in_specs=[
            pl.BlockSpec((tm, ha.shape[1]), lambda j, i: (i, 0)),
            pl.BlockSpec((tm, ob.shape[1]), lambda j, i: (i, 0)),
            pl.BlockSpec((wa.shape[0], tn), lambda j, i: (0, j)),
            pl.BlockSpec((wb.shape[0], tn), lambda j, i: (0, j)),
            pl.BlockSpec((tm, tn), lambda j, i: (i, _Z2_GA // tn + j)),
            pl.BlockSpec((tm, tn), lambda j, i: (i, _Z2_GBM // tn + j)),
        ],
        out_specs=pl.BlockSpec((tm, tn), lambda j, i: (i, j)),
        out_shape=jax.ShapeDtypeStruct((m, d), BF16),
        compiler_params=_params(("parallel", "parallel")),
        name="merge",
    )(ha, ob, wa, wb, z2, z2)


def _wo_router_body(n_first, mg_ref, wo_ref, xa_ref, xb_ref, g2_ref, wrh_ref, wrl_ref, br_ref,
                    x1_ref, h2_ref, idx_ref, wt_ref):
    def run(x_ref):
        x1 = x_ref[...] + _dot(mg_ref[...], wo_ref[...])
        x1_ref[...] = x1
        h2 = x1 * lax.rsqrt(jnp.mean(x1 * x1, axis=-1, keepdims=True) + EPS) * g2_ref[...]
        h2_ref[...] = h2
        logits = _dot3(h2, wrh_ref[...], wrl_ref[...]) + br_ref[...]
        lane = lax.broadcasted_iota(jnp.int32, logits.shape, 1)
        cur = jnp.where(lane < N_EXPERTS, logits, -jnp.inf)
        vals, idxs = [], []
        for _ in range(TOP_K):
            mx = jnp.max(cur, axis=1, keepdims=True)
            ix = jnp.min(jnp.where(cur == mx, lane, LANES), axis=1, keepdims=True)
            vals.append(mx)
            idxs.append(ix)
            cur = jnp.where(lane == ix, -jnp.inf, cur)
        exps = [jnp.exp(v - vals[0]) for v in vals]
        tot = exps[0] + exps[1] + exps[2] + exps[3]
        idx_out = jnp.zeros(logits.shape, jnp.int32)
        wt_out = jnp.zeros(logits.shape, F32)
        for kk in range(TOP_K):
            idx_out = jnp.where(lane == kk, idxs[kk], idx_out)
            wt_out = jnp.where(lane == kk, exps[kk] / tot, wt_out)
        idx_ref[...] = idx_out
        wt_ref[...] = wt_out

    i = pl.program_id(0)
    pl.when(i < n_first)(lambda: run(xa_ref))
    pl.when(i >= n_first)(lambda: run(xb_ref))


def _wo_router(merged, wo, xa, xb, g2, wr_hi, wr_lo, br, tm):
    m, d = merged.shape
    n_first = xa.shape[0] // tm
    return pl.pallas_call(
        functools.partial(_wo_router_body, n_first),
        grid=(m // tm,),
        in_specs=[
            pl.BlockSpec((tm, d), lambda i: (i, 0)),
            pl.BlockSpec((d, d), lambda i: (0, 0)),
        ] + _two_group_specs(tm, d, n_first) + [
            pl.BlockSpec((1, d), lambda i: (0, 0)),
            pl.BlockSpec((d, LANES), lambda i: (0, 0)),
            pl.BlockSpec((d, LANES), lambda i: (0, 0)),
            pl.BlockSpec((1, LANES), lambda i: (0, 0)),
        ],
        out_specs=[
            pl.BlockSpec((tm, d), lambda i: (i, 0)),
            pl.BlockSpec((tm, d), lambda i: (i, 0)),
            pl.BlockSpec((tm, LANES), lambda i: (i, 0)),
            pl.BlockSpec((tm, LANES), lambda i: (i, 0)),
        ],
        out_shape=[
            jax.ShapeDtypeStruct((m, d), F32),
            jax.ShapeDtypeStruct((m, d), F32),
            jax.ShapeDtypeStruct((m, LANES), jnp.int32),
            jax.ShapeDtypeStruct((m, LANES), F32),
        ],
        compiler_params=_params(("parallel",)),
        name="wo_router",
    )(merged, wo, xa, xb, g2, wr_hi, wr_lo, br)


def _dispatch_body(dest_ref, h_ref, xs_hbm, sem):
    tm = h_ref.shape[0]

    def row_copy(t, kk):
        return pltpu.make_async_copy(h_ref.at[pl.ds(t, 1)],
                                     xs_hbm.at[pl.ds(dest_ref[0, 0, t * TOP_K + kk], 1)], sem)

    def start(t, carry):
        for kk in range(TOP_K):
            row_copy(t, kk).start()
        return carry
    lax.fori_loop(0, tm, start, 0)

    def wait(t, carry):
        for kk in range(TOP_K):
            row_copy(t, kk).wait()
        return carry
    lax.fori_loop(0, tm, wait, 0)


def _dispatch(dest3, h2, tm):
    n, d = h2.shape
    return pl.pallas_call(
        _dispatch_body,
        grid=(n // tm,),
        in_specs=[
            pl.BlockSpec((1, 1, tm * TOP_K), lambda i: (i, 0, 0), memory_space=pltpu.SMEM),
            pl.BlockSpec((tm, d), lambda i: (i, 0)),
        ],
        out_specs=pl.BlockSpec(memory_space=pl.ANY),
        out_shape=jax.ShapeDtypeStruct((n * TOP_K, d), h2.dtype),
        scratch_shapes=[pltpu.SemaphoreType.DMA(())],
        compiler_params=_params(("arbitrary",)),
        name="moe_dispatch",
    )(dest3, h2)


def _item_flags(t, ie_ref, fb_ref, ni_ref):
    active = t < ni_ref[0]
    new_expert = jnp.logical_or(t == 0, ie_ref[t] != ie_ref[jnp.maximum(t - 1, 0)])
    return active, jnp.logical_and(active, new_expert), fb_ref[t] == 1


def _row_mask(rows, t, lo_ref, hi_ref):
    row = lax.broadcasted_iota(jnp.int32, (rows, 1), 0)
    return jnp.logical_and(row >= lo_ref[t], row < hi_ref[t])


def _stream_expert_weights(t, f, new_expert, ie_ref, grp_ref, gexp_ref, ng_ref, hbm_refs, land_refs, bf_refs, sem):
    tn = bf_refs[0].shape[1]

    def copies(e, ff):
        col = pl.multiple_of(ff * tn, tn)
        return [pltpu.make_async_copy(hbm.at[e, :, pl.ds(col, tn)], land, sem.at[k])
                for k, (hbm, land) in enumerate(zip(hbm_refs, land_refs))]

    @pl.when(jnp.logical_and(t == 0, f == 0))
    def _():
        for cp in copies(ie_ref[0], 0):
            cp.start()

    @pl.when(new_expert)
    def _():
        for cp in copies(ie_ref[t], f):
            cp.wait()
        for land, bf in zip(land_refs, bf_refs):
            bf[...] = land[...].astype(BF16)
        g = grp_ref[t]
        last = g + 1 >= ng_ref[0]
        nxt_e = gexp_ref[jnp.where(last, 0, g + 1)]
        nxt_f = jnp.where(last, f + 1, f)

        @pl.when(jnp.logical_or(jnp.logical_not(last), f + 1 < pl.num_programs(0)))
        def _():
            for cp in copies(nxt_e, nxt_f):
                cp.start()


def _expert_up_body(ib_ref, ie_ref, lo_ref, hi_ref, fb_ref, ni_ref, grp_ref, gexp_ref, ng_ref,
                    x_ref, wg_hbm, wu_hbm, bg_ref, bu_ref, a_ref, wg_l, wu_l, wg_s, wu_s, sem):
    f = pl.program_id(0)
    t = pl.program_id(1)
    active, new_expert, first_visit = _item_flags(t, ie_ref, fb_ref, ni_ref)
    _stream_expert_weights(t, f, new_expert, ie_ref, grp_ref, gexp_ref, ng_ref,
                           (wg_hbm, wu_hbm), (wg_l, wu_l), (wg_s, wu_s), sem)

    def compute():
        x = x_ref[...].astype(BF16)
        g = jnp.minimum(_dot(x, wg_s[...]) + bg_ref[0], SWIGLU_LIMIT)
        u = jnp.clip(_dot(x, wu_s[...]) + bu_ref[0], -SWIGLU_LIMIT, SWIGLU_LIMIT)
        return (g * jax.nn.sigmoid(SWIGLU_ALPHA * g) * (u + 1.0)).astype(BF16)

    mask = _row_mask(x_ref.shape[0], t, lo_ref, hi_ref)

    @pl.when(jnp.logical_and(active, first_visit))
    def _():
        a_ref[...] = jnp.where(mask, compute(), jnp.zeros(a_ref.shape, BF16))

    @pl.when(jnp.logical_and(active, jnp.logical_not(first_visit)))
    def _():
        a_ref[...] = jnp.where(mask, compute(), a_ref[...])


def _item_spec(shape, fn):
    def index_map(f, t, ib, ie, lo, hi, fb, ni, *_):
        tc = jnp.minimum(t, ni[0] - 1)
        return fn(tc, f, ib, ie)
    return pl.BlockSpec(shape, index_map)


_N_TABLES = 9


def _expert_up(tables, xs, w_gate, w_up, b_gate, b_up, rows, tf):
    p, d = xs.shape
    dff = w_gate.shape[2]
    n_items = tables[0].shape[0]
    grid_spec = pltpu.PrefetchScalarGridSpec(
        num_scalar_prefetch=_N_TABLES,
        grid=(dff // tf, n_items),
        in_specs=[
            _item_spec((rows, d), lambda t, f, ib, ie: (ib[t], 0)),
            pl.BlockSpec(memory_space=pl.ANY),
            pl.BlockSpec(memory_space=pl.ANY),
            _item_spec((1, 1, tf), lambda t, f, ib, ie: (ie[t], 0, f)),
            _item_spec((1, 1, tf), lambda t, f, ib, ie: (ie[t], 0, f)),
        ],
        out_specs=_item_spec((rows, tf), lambda t, f, ib, ie: (ib[t], f)),
        scratch_shapes=[pltpu.VMEM((d, tf), F32), pltpu.VMEM((d, tf), F32),
                        pltpu.VMEM((d, tf), BF16), pltpu.VMEM((d, tf), BF16),
                        pltpu.SemaphoreType.DMA((2,))],
    )
    return pl.pallas_call(
        _expert_up_body,
        grid_spec=grid_spec,
        out_shape=jax.ShapeDtypeStruct((p, dff), BF16),
        compiler_params=_params(("arbitrary", "arbitrary")),
        name="expert_up",
    )(*tables, xs, w_gate, w_up, b_gate, b_up)


def _expert_down_body(ib_ref, ie_ref, lo_ref, hi_ref, fb_ref, ni_ref, grp_ref, gexp_ref, ng_ref,
                      a_ref, wd_hbm, bd_ref, o_ref, wd_l, wd_s, sem):
    f = pl.program_id(0)
    t = pl.program_id(1)
    active, new_expert, first_visit = _item_flags(t, ie_ref, fb_ref, ni_ref)
    _stream_expert_weights(t, f, new_expert, ie_ref, grp_ref, gexp_ref, ng_ref,
                           (wd_hbm,), (wd_l,), (wd_s,), sem)

    def compute():
        return _dot(a_ref[...], wd_s[...]) + bd_ref[0]

    mask = _row_mask(a_ref.shape[0], t, lo_ref, hi_ref)

    @pl.when(jnp.logical_and(active, first_visit))
    def _():
        o_ref[...] = jnp.where(mask, compute(), 0.0)

    @pl.when(jnp.logical_and(active, jnp.logical_not(first_visit)))
    def _():
        o_ref[...] = jnp.where(mask, compute(), o_ref[...])


def _expert_down(tables, a, w_down, b_down, rows, tn):
    p, dff = a.shape
    d = w_down.shape[2]
    n_items = tables[0].shape[0]
    grid_spec = pltpu.PrefetchScalarGridSpec(
        num_scalar_prefetch=_N_TABLES,
        grid=(d // tn, n_items),
        in_specs=[
            _item_spec((rows, dff), lambda t, f, ib, ie: (ib[t], 0)),
            pl.BlockSpec(memory_space=pl.ANY),
            _item_spec((1, 1, tn), lambda t, f, ib, ie: (ie[t], 0, f)),
        ],
        out_specs=_item_spec((rows, tn), lambda t, f, ib, ie: (ib[t], f)),
        scratch_shapes=[pltpu.VMEM((dff, tn), F32), pltpu.VMEM((dff, tn), BF16),
                        pltpu.SemaphoreType.DMA((1,))],
    )
    return pl.pallas_call(
        _expert_down_body,
        grid_spec=grid_spec,
        out_shape=jax.ShapeDtypeStruct((p, d), F32),
        compiler_params=_params(("arbitrary", "arbitrary")),
        name="expert_down",
    )(*tables, a, w_down, b_down)


def _combine_body(dest_ref, wt_ref, x1_ref, fn_ref, rows_hbm, y_ref, buf, sem):
    tm = x1_ref.shape[0]

    def row_copy(t, kk):
        return pltpu.make_async_copy(rows_hbm.at[pl.ds(dest_ref[0, 0, t * TOP_K + kk], 1)],
                                     buf.at[kk, pl.ds(t, 1)], sem)

    def start(t, carry):
        for kk in range(TOP_K):
            row_copy(t, kk).start()
        return carry
    lax.fori_loop(0, tm, start, 0)

    def wait(t, carry):
        for kk in range(TOP_K):
            row_copy(t, kk).wait()
        return carry
    lax.fori_loop(0, tm, wait, 0)

    wt = wt_ref[...]
    moe = wt[:, 0:1] * buf[0]
    for kk in range(1, TOP_K):
        moe = moe + wt[:, kk:kk + 1] * buf[kk]
    y = x1_ref[...] + moe
    y_ref[...] = y * lax.rsqrt(jnp.mean(y * y, axis=-1, keepdims=True) + EPS) * fn_ref[...]


def _combine(dest3, wts, x1, final_norm, out_rows, *, tile0, ntiles, tm):
    d = x1.shape[1]
    return pl.pallas_call(
        _combine_body,
        grid=(ntiles,),
        in_specs=[
            pl.BlockSpec((1, 1, tm * TOP_K), lambda i: (tile0 + i, 0, 0), memory_space=pltpu.SMEM),
            pl.BlockSpec((tm, LANES), lambda i: (tile0 + i, 0)),
            pl.BlockSpec((tm, d), lambda i: (tile0 + i, 0)),
            pl.BlockSpec((1, d), lambda i: (0, 0)),
            pl.BlockSpec(memory_space=pl.ANY),
        ],
        out_specs=pl.BlockSpec((tm, d), lambda i: (i, 0)),
        out_shape=jax.ShapeDtypeStruct((ntiles * tm, d), F32),
        scratch_shapes=[pltpu.VMEM((TOP_K, tm, d), F32), pltpu.SemaphoreType.DMA(())],
        compiler_params=_params(("arbitrary",)),
        name="moe_combine",
    )(dest3, wts, x1, final_norm, out_rows)


def _routing_tables(top_idx, rows):
    n = top_idx.shape[0]
    m = n * TOP_K
    experts = jnp.arange(N_EXPERTS, dtype=jnp.int32)
    e_flat = top_idx.reshape(m)
    onehot = (e_flat[:, None] == experts[None, :]).astype(jnp.int32)
    csum = jnp.cumsum(onehot, axis=0)
    counts = csum[-1]
    ends = jnp.cumsum(counts)
    starts = ends - counts
    dest = jnp.sum(onehot * (csum - 1 + starts[None, :]), axis=1)

    first_blk = starts // rows
    last_blk = (ends - 1) // rows
    n_it = jnp.where(counts > 0, last_blk - first_blk + 1, 0)
    it_end = jnp.cumsum(n_it)
    it_start = it_end - n_it
    n_items = it_end[-1:]
    max_items = m // rows + N_EXPERTS
    t = jnp.minimum(jnp.arange(max_items, dtype=jnp.int32), n_items[0] - 1)
    ie = jnp.sum((it_end[None, :] <= t[:, None]).astype(jnp.int32), axis=1)
    sel = (ie[:, None] == experts[None, :]).astype(jnp.int32)
    pick = lambda v: jnp.sum(sel * v[None, :], axis=1)
    ib = pick(first_blk) + t - pick(it_start)
    lo = jnp.clip(pick(starts) - ib * rows, 0, rows)
    hi = jnp.clip(pick(ends) - ib * rows, 0, rows)
    fb = jnp.concatenate([jnp.ones((1,), jnp.int32), (ib[1:] != ib[:-1]).astype(jnp.int32)])
    has = (counts > 0).astype(jnp.int32)
    pos = jnp.cumsum(has) - 1
    grp = pick(pos)
    gexp = jnp.sum(((pos[None, :] == experts[:, None]) & (has[None, :] == 1)).astype(jnp.int32) * experts[None, :],
                   axis=1)
    n_groups = jnp.sum(has, keepdims=True)
    i32 = lambda v: v.astype(jnp.int32)
    return i32(dest), (i32(ib), i32(ie), i32(lo), i32(hi), fb, i32(n_items), i32(grp), i32(gexp), i32(n_groups))


def _pad_rows8(a, at):
    b, r, n = a.shape
    return jnp.zeros((b, SUBLANES, n), a.dtype).at[:, at:at + r].set(a)


def _rope_tables(pos, half):
    inv = jnp.power(ROPE_BASE, -jnp.arange(half, dtype=F32) / half)
    ang = pos.astype(F32)[:, None] * inv[None, :]
    return jnp.cos(ang), jnp.sin(ang)


def kernel(x_prompt, x_sample, state_mlstm_C, state_mlstm_n, state_mlstm_m, state_mlstm_conv, state_ret,
           meta_tokens, norm1, w_in, b_igate, b_fgate, conv_w, conv_b, mlstm_norm, ret_norm,
           w_out_a, w_out_b, w_o, norm2, w_router, b_router, w_gate, b_gate, w_up, b_up,
           w_down, b_down, final_norm):
    bp, tp, d = x_prompt.shape
    bs, ts, _ = x_sample.shape
    assert norm1.shape[0] == 1, "single-layer stack"
    n_p, n_s = bp * tp, bs * ts
    n_tok = n_p + n_s
    qk_cols = 2 * H_A * _DK
    gate_col0 = qk_cols + H_A * _DV

    w_in0 = w_in[0]
    rest_col0 = gate_col0 + 2 * H_A
    rest_cols = w_in0.shape[1] - rest_col0
    w_if =jnp.zeros((d, LANES), F32).at[:, :2 * H_A].set(w_in0[:, gate_col0:gate_col0 + 2 * H_A])
    w_if_hi, w_if_lo = _split_bf16(w_if)
    b_if = jnp.zeros((1, LANES), F32).at[0, :H_A].set(b_igate[0]).at[0, H_A:2 * H_A].set(b_fgate[0])
    w_r = jnp.zeros((d, LANES), F32).at[:, :N_EXPERTS].set(w_router[0])
    w_r_hi, w_r_lo = _split_bf16(w_r)
    b_r = jnp.zeros((1, LANES), F32).at[0, :N_EXPERTS].set(b_router[0])
    conv_w8 = jnp.zeros((SUBLANES, qk_cols), F32).at[:CONV_W].set(conv_w[0])
    conv_b1 = conv_b[0].reshape(1, qk_cols)
    g1 = norm1[0].reshape(1, d)
    g2 = norm2[0].reshape(1, d)
    gf = final_norm.reshape(1, d)
    mnorm = mlstm_norm[0].reshape(1, H_A * _DV)
    rnorm = ret_norm[0].reshape(1, H_B * _DV)
    log_g = jnp.asarray(np.log1p(-np.power(2.0, -5.0 - np.arange(H_B, dtype=np.float32))).astype(np.float32))

    xp2 = x_prompt.reshape(n_p, d)
    xs2 = x_sample.reshape(n_s, d)
    tm = next(t for t in (512, 256, 128) if n_p % t == 0 and n_s % t == 0)
    h_all, gates = _norm_gates(xp2, xs2, g1, w_if_hi, w_if_lo, b_if, tm)
    z1 = _in_proj(h_all, w_in0, tm, 1024, 0, gate_col0)
    z2 = _in_proj(h_all, w_in0, tm, 1024, rest_col0, rest_cols)
    h_meta, gates_meta = _norm_gates(meta_tokens, None, g1, w_if_hi, w_if_lo, b_if, N_META)
    z1_meta = _in_proj(h_meta, w_in0, N_META, 1024, 0, gate_col0)
    z2_meta = _in_proj(h_meta, w_in0, N_META, 1024, rest_col0, rest_cols)

    def gate_rows(g, nb, t):
        return jnp.transpose(g[:, :SUBLANES].reshape(nb, t, SUBLANES), (0, 2, 1))

    zeros_c = jnp.zeros((1, H_A, _DK, _DV), F32)
    zeros_n = jnp.zeros((1, H_A, 1, _DK), F32)
    zeros_m = jnp.zeros((1, H_A, SUBLANES, LANES), F32)
    zeros_conv = jnp.zeros((1, SUBLANES, qk_cols), F32)
    _, c_meta, n_meta, m_meta = _mlstm_scan(
        z1_meta, z2_meta, gates_meta, gate_rows(gates_meta, 1, N_META), zeros_conv, conv_w8, conv_b1, mnorm,
        zeros_c, zeros_n, zeros_m, row0=0, nb=1, t=N_META, c=N_META,
        share_state=False, ha_prev=None, m_total=N_META)
    conv_meta = _pad_rows8(z1_meta[None, N_META - (CONV_W - 1):, :qk_cols], SUBLANES - (CONV_W - 1))
    cp = PROMPT_CHUNK if tp % PROMPT_CHUNK == 0 else tp
    ha, p_c, p_n, p_m = _mlstm_scan(
        z1, z2, gates, gate_rows(gates[:n_p], bp, tp), conv_meta, conv_w8, conv_b1, mnorm,
        c_meta, n_meta, m_meta, row0=0, nb=bp, t=tp, c=cp,
        share_state=True, ha_prev=None, m_total=n_tok)
    conv_s0 = _pad_rows8(state_mlstm_conv[0], SUBLANES - (CONV_W - 1))
    m_s0 = jnp.broadcast_to(state_mlstm_m[0][:, :, None, None], (bs, H_A, SUBLANES, LANES))
    ha, s_c, s_n, s_m = _mlstm_scan(
        z1, z2, gates, gate_rows(gates[n_p:], bs, ts), conv_s0, conv_w8, conv_b1, mnorm,
        state_mlstm_C[0], state_mlstm_n[0].reshape(bs, H_A, 1, _DK), m_s0,
        row0=n_p, nb=bs, t=ts, c=ts, share_state=False, ha_prev=ha, m_total=n_tok)

    cos_m, sin_m = _rope_tables(jnp.arange(N_META, dtype=jnp.int32), _DK // 2)
    cos_p, sin_p = _rope_tables(N_META + jnp.arange(tp, dtype=jnp.int32), _DK // 2)
    cos_s, sin_s = _rope_tables(N_META + PAST_LEN + jnp.arange(ts, dtype=jnp.int32), _DK // 2)
    zeros_s = jnp.zeros((1, H_B, _DK, _DV), F32)
    _, s_meta = _ret_scan(log_g, z2_meta, cos_m, sin_m, rnorm, zeros_s, row0=0, nb=1,
                          t=N_META, c=N_META, share_state=False, ob_prev=None, m_total=N_META)
    ob, p_ret = _ret_scan(log_g, z2, cos_p, sin_p, rnorm, s_meta, row0=0, nb=bp, t=tp, c=cp,
                          share_state=True, ob_prev=None, m_total=n_tok)
    ob, s_ret = _ret_scan(log_g, z2, cos_s, sin_s, rnorm, state_ret[0], row0=n_p, nb=bs,
                          t=ts, c=ts, share_state=False, ob_prev=ob, m_total=n_tok)

    merged = _merge(ha, ob, w_out_a[0].astype(BF16), w_out_b[0].astype(BF16), z2, tm, 1024)
    x1, h2, top_idx, wts = _wo_router(merged, w_o[0].astype(BF16), xp2, xs2, g2, w_r_hi, w_r_lo, b_r,
                                      min(tm, 256))

    rows = EXPERT_ROWS
    dest, tables = _routing_tables(top_idx[:, :TOP_K], rows)
    td = DISPATCH_ROWS
    dest3 = dest.reshape(n_tok // td, 1, td * TOP_K)
    xs = _dispatch(dest3, h2, td)
    dff = w_gate.shape[3]
    act = _expert_up(tables, xs, w_gate[0], w_up[0],
                     b_gate[0].reshape(N_EXPERTS, 1, dff), b_up[0].reshape(N_EXPERTS, 1, dff), rows, 1024)
    out_rows = _expert_down(tables, act, w_down[0], b_down[0].reshape(N_EXPERTS, 1, d), rows, 1024)
    y_p = _combine(dest3, wts, x1, gf, out_rows, tile0=0, ntiles=n_p // td, tm=td)
    y_s = _combine(dest3, wts, x1, gf, out_rows, tile0=n_p // td, ntiles=n_s // td, tm=td)

    def tail_rows(row0, nb, t):
        return jnp.stack([z1[row0 + (b + 1) * t - (CONV_W - 1):row0 + (b + 1) * t, :qk_cols] for b in range(nb)])

    p_conv = tail_rows(0, bp, tp)
    s_conv = tail_rows(n_p, bs, ts)
    return (y_p.reshape(bp, tp, d), y_s.reshape(bs, ts, d),
            p_c[None], p_n.reshape(1, bp, H_A, _DK), p_m[None, :, :, 0, 0], p_conv[None], p_ret[None],
            s_c[None], s_n.reshape(1, bs, H_A, _DK), s_m[None, :, :, 0, 0], s_conv[None], s_ret[None])
```
